```python
import math
import jax
import jax.numpy as jnp
from jax import lax
import numpy as np

D_MODEL = 1024
BATCH = 1
SEQ = 16384
DEPTH = 2

N_META = 16
A_HEADS = 8
A_DH = 64
A_DV = 2 * A_DH
Q_BLOCK = 128
DN_HEADS = 8
DN_DK = 128
DN_DV = 128
DN_CONV = 4
CHUNK = 64
D_FF = 2816
FFN_CONV = 3
RMS_EPS = 1e-6
ALIBI_MAX_EXP = 8.0

A_Q = A_HEADS * 2 * A_DH
A_K = A_HEADS * 2 * A_DH
A_V = A_HEADS * A_DV
DN_QKV = DN_HEADS * (2 * DN_DK + DN_DV)
DN_Z = DN_HEADS * DN_DV
IN_WIDTHS = (A_Q, A_K, A_V, DN_QKV, DN_Z, DN_HEADS, DN_HEADS, D_MODEL, D_MODEL)
D_IN = A_Q + A_K + A_V + DN_QKV + DN_Z + 2 * DN_HEADS + 2 * D_MODEL

kernel_name = 'hybrid_diffattn_gdn_convffn'


def rms_norm(x, g, eps=RMS_EPS):
    xf = x.astype(jnp.float32)
    y = xf * lax.rsqrt(jnp.mean(xf * xf, axis=-1, keepdims=True) + eps)
    return (y * g.astype(jnp.float32)).astype(x.dtype)


def l2_normalize(x, eps=1e-6):
    xf = x.astype(jnp.float32)
    return xf * lax.rsqrt(jnp.sum(xf * xf, axis=-1, keepdims=True) + eps)


def causal_dwconv(x, w):
    k_width = w.shape[0]
    return lax.conv_general_dilated(
        x, w[:, None, :].astype(x.dtype), window_strides=(1,), padding=((k_width - 1, 0),),
        dimension_numbers=('NWC', 'WIO', 'NWC'), feature_group_count=x.shape[-1])


def diff_attention(q, k, v, lam, slopes):
    B, L = q.shape[0], q.shape[1]
    n_blk = -(-L // Q_BLOCK)
    lq = n_blk * Q_BLOCK
    qp = jnp.pad(q, ((0, 0), (0, lq - L), (0, 0), (0, 0), (0, 0)))
    qb = jnp.moveaxis(qp.reshape((B, n_blk, Q_BLOCK) + q.shape[2:]), 1, 0)
    kpos = jnp.arange(L, dtype=jnp.int32)
    scale = A_DH ** -0.5

    def one_block(args):
        q_blk, start = args
        qpos = start + jnp.arange(Q_BLOCK, dtype=jnp.int32)
        s = jnp.einsum('bqhcd,bkhcd->bhcqk', q_blk, k).astype(jnp.float32) * scale
        dist = (qpos[:, None] - kpos[None, :]).astype(jnp.float32)
        bias = -slopes[:, None, None, None] * dist
        s = jnp.where(dist >= 0, s + bias, -jnp.inf)
        p = jax.nn.softmax(s, axis=-1)
        p = p[:, :, 0] - lam * p[:, :, 1]
        return jnp.einsum('bhqk,bkhe->bqhe', p.astype(v.dtype), v)

    starts = jnp.arange(n_blk, dtype=jnp.int32) * Q_BLOCK
    o = lax.map(one_block, (qb, starts))
    o = jnp.moveaxis(o, 0, 1).reshape(B, lq, q.shape[2], v.shape[-1])
    return o[:, :L]


def gated_delta_rule(q, k, v, beta, g):
    B, L, H, DK = q.shape
    front = (CHUNK - N_META % CHUNK) % CHUNK
    back = (-(L + front)) % CHUNK
    lp = L + front + back
    n = lp // CHUNK

    def prep(t):
        pad = ((0, 0), (front, back)) + ((0, 0),) * (t.ndim - 2)
        t = jnp.pad(t, pad).reshape((B, n, CHUNK) + t.shape[2:])
        return jnp.moveaxis(t, 3, 1)

    qc = prep(q * DK ** -0.5)
    kc = prep(k)
    vc = prep(v)
    bc = prep(beta)
    gc = prep(g)
    idx = jnp.arange(CHUNK)
    lower_incl = idx[:, None] >= idx[None, :]
    strict = idx[:, None] > idx[None, :]
    G = jnp.cumsum(gc, axis=-1)
    decay = jnp.exp(jnp.where(lower_incl, G[..., :, None] - G[..., None, :], -jnp.inf))
    kb = kc * bc[..., None]
    vb = vc * bc[..., None]
    A = jnp.where(strict, jnp.einsum('bhnid,bhnjd->bhnij', kb, kc) * decay, 0.0)
    T = A + jnp.eye(CHUNK, dtype=A.dtype)
    u = lax.linalg.triangular_solve(T, vb, left_side=True, lower=True, unit_diagonal=True)
    w = lax.linalg.triangular_solve(T, kb * jnp.exp(G)[..., None], left_side=True, lower=True,
                                    unit_diagonal=True)
    qk = jnp.where(lower_incl, jnp.einsum('bhnid,bhnjd->bhnij', qc, kc) * decay, 0.0)
    q_dec = qc * jnp.exp(G)[..., None]
    k_dec = kc * jnp.exp(G[..., -1:] - G)[..., None]
    chunk_dec = jnp.exp(G[..., -1])

    def step(S, xs):
        u_n, w_n, qd_n, qk_n, kd_n, cd_n = xs
        v_new = u_n - jnp.einsum('bhcd,bhde->bhce', w_n, S)
        o = jnp.einsum('bhcd,bhde->bhce', qd_n, S) + jnp.einsum('bhij,bhje->bhie', qk_n, v_new)
        S = S * cd_n[..., None, None] + jnp.einsum('bhcd,bhce->bhde', kd_n, v_new)
        return S, o

    xs = (jnp.moveaxis(u, 2, 0), jnp.moveaxis(w, 2, 0), jnp.moveaxis(q_dec, 2, 0),
          jnp.moveaxis(qk, 2, 0), jnp.moveaxis(k_dec, 2, 0), jnp.moveaxis(chunk_dec, 2, 0))
    S0 = jnp.zeros((B, H, DK, v.shape[-1]), jnp.float32)
    _, o = lax.scan(step, S0, xs)
    o = o.transpose(1, 0, 3, 2, 4).reshape(B, lp, H, v.shape[-1])
    return o[:, front:front + L]


def setup_inputs(seed: int = 0) -> dict:
    key = jax.random.key(seed)
    ks = jax.random.split(key, 24)
    f32 = jnp.float32

    def normal(k, shape, scale):
        return jax.random.normal(k, shape, f32) * scale

    def gain(k, shape):
        return 1.0 + 0.01 * jax.random.normal(k, shape, f32)

    dt = jnp.exp(jax.random.uniform(ks[13], (DEPTH, DN_HEADS), f32, math.log(1e-3), math.log(1e-1)))
    return {
        'x': normal(ks[0], (BATCH, SEQ, D_MODEL), 1.0),
        'meta_tokens': normal(ks[1], (N_META, D_MODEL), 1.0),
        'mix_norm_g': gain(ks[2], (DEPTH, D_MODEL)),
        'w_in': normal(ks[3], (DEPTH, D_MODEL, D_IN), D_MODEL ** -0.5),
        'q_norm_g': gain(ks[4], (DEPTH, A_DH)),
        'k_norm_g': gain(ks[5], (DEPTH, A_DH)),
        'lambda_q1': normal(ks[6], (DEPTH, A_DH), 0.1),
        'lambda_k1': normal(ks[7], (DEPTH, A_DH), 0.1),
        'lambda_q2': normal(ks[8], (DEPTH, A_DH), 0.1),
        'lambda_k2': normal(ks[9], (DEPTH, A_DH), 0.1),
        'attn_subln_g': gain(ks[10], (DEPTH, A_DV)),
        'dn_conv_w': normal(ks[11], (DEPTH, DN_CONV, DN_QKV), DN_CONV ** -0.5),
        'dn_a_log': jnp.log(jax.random.uniform(ks[12], (DEPTH, DN_HEADS), f32, 1.0, 16.0)),
        'dn_dt_bias': dt + jnp.log(-jnp.expm1(-dt)),
        'dn_norm_g': gain(ks[14], (DEPTH, DN_DV)),
        'w_branch_attn': normal(ks[15], (DEPTH, A_HEADS * A_DV, D_MODEL), (A_HEADS * A_DV) ** -0.5),
        'w_branch_dn': normal(ks[16], (DEPTH, DN_HEADS * DN_DV, D_MODEL), (DN_HEADS * DN_DV) ** -0.5),
        'w_out': normal(ks[17], (DEPTH, D_MODEL, D_MODEL), D_MODEL ** -0.5),
        'ffn_norm_g': gain(ks[18], (DEPTH, D_MODEL)),
        'w_ffn_up': normal(ks[19], (DEPTH, D_MODEL, 2 * D_FF), D_MODEL ** -0.5),
        'ffn_conv_w': normal(ks[20], (DEPTH, FFN_CONV, 2 * D_FF), FFN_CONV ** -0.5),
        'w_ffn_down': normal(ks[21], (DEPTH, D_FF, D_MODEL), D_FF ** -0.5),
    }


def reference(x, meta_tokens, mix_norm_g, w_in, q_norm_g, k_norm_g, lambda_q1, lambda_k1,
              lambda_q2, lambda_k2, attn_subln_g, dn_conv_w, dn_a_log, dn_dt_bias, dn_norm_g,
              w_branch_attn, w_branch_dn, w_out, ffn_norm_g, w_ffn_up, ffn_conv_w, w_ffn_down):
    B = x.shape[0]
    meta = jnp.broadcast_to(meta_tokens[None].astype(x.dtype), (B, N_META, D_MODEL))
    h = jnp.concatenate([meta, x], axis=1)
    L = h.shape[1]
    slopes = jnp.exp2(-ALIBI_MAX_EXP / A_HEADS * jnp.arange(1, A_HEADS + 1, dtype=jnp.float32))
    splits = []
    acc = 0
    for wdt in IN_WIDTHS[:-1]:
        acc += wdt
        splits.append(acc)

    for layer in range(DEPTH):
        lam_init = 0.8 - 0.6 * math.exp(-0.3 * layer)
        u = rms_norm(h, mix_norm_g[layer])
        proj = u @ w_in[layer]
        aq, ak, av, dqkv, dz, db, da, ga, gb = jnp.split(proj, splits, axis=-1)

        aq = rms_norm(aq.reshape(B, L, A_HEADS, 2, A_DH), q_norm_g[layer])
        ak = rms_norm(ak.reshape(B, L, A_HEADS, 2, A_DH), k_norm_g[layer])
        av = av.reshape(B, L, A_HEADS, A_DV)
        lam = (jnp.exp(jnp.sum(lambda_q1[layer].astype(jnp.float32) * lambda_k1[layer].astype(jnp.float32)))
               - jnp.exp(jnp.sum(lambda_q2[layer].astype(jnp.float32) * lambda_k2[layer].astype(jnp.float32)))
               + lam_init)
        ao = diff_attention(aq, ak, av, lam, slopes)
        ao = rms_norm(ao, attn_subln_g[layer]) * (1.0 - lam_init)
        ya = ao.reshape(B, L, A_HEADS * A_DV).astype(h.dtype) @ w_branch_attn[layer]

        dqkv = jax.nn.silu(causal_dwconv(dqkv, dn_conv_w[layer]))
        dq, dk, dv = jnp.split(dqkv, [DN_HEADS * DN_DK, 2 * DN_HEADS * DN_DK], axis=-1)
        dq = l2_normalize(dq.reshape(B, L, DN_HEADS, DN_DK))
        dk = l2_normalize(dk.reshape(B, L, DN_HEADS, DN_DK))
        dv = dv.reshape(B, L, DN_HEADS, DN_DV).astype(jnp.float32)
        beta = jax.nn.sigmoid(db.astype(jnp.float32))
        gdec = -jnp.exp(dn_a_log[layer].astype(jnp.float32)) * jax.nn.softplus(
            da.astype(jnp.float32) + dn_dt_bias[layer].astype(jnp.float32))
        do = gated_delta_rule(dq, dk, dv, beta, gdec)
        do = rms_norm(do, dn_norm_g[layer]) * jax.nn.silu(
            dz.reshape(B, L, DN_HEADS, DN_DV).astype(jnp.float32))
        yb = do.reshape(B, L, DN_HEADS * DN_DV).astype(h.dtype) @ w_branch_dn[layer]

        mixed = jax.nn.sigmoid(ga) * ya + jax.nn.sigmoid(gb) * yb
        h = h + mixed @ w_out[layer]

        f = rms_norm(h, ffn_norm_g[layer]) @ w_ffn_up[layer]
        f = causal_dwconv(f, ffn_conv_w[layer])
        f_gate, f_up = jnp.split(f, 2, axis=-1)
        h = h + (jax.nn.silu(f_gate) * f_up) @ w_ffn_down[layer]

    return h[:, N_META:]
```

```python
import functools
import math

import jax
import jax.numpy as jnp
from jax import lax
from jax.experimental import pallas as pl
from jax.experimental.pallas import tpu as pltpu

D_MODEL = 1024
N_META = 16
A_HEADS = 8
A_DH = 64
A_DV = 2 * A_DH
DN_HEADS = 8
DN_DK = 128
DN_DV = 128
DN_CONV = 4
D_FF = 2816
FFN_CONV = 3
RMS_EPS = 1e-6
L2_EPS = 1e-6
ALIBI_MAX_EXP = 8.0

LANES = 128
SUBLANES = 8
ROW_ALIGN = 256
ATTN_TQ = 256
ATTN_TK_BIG = 1024
GDN_CHUNK = 128
BLK = 1024
N_PROJ_BLK = 9
LOG2E = 1.4426950408889634
VMEM_LIMIT = 52 * 1024 * 1024

F32 = jnp.float32
BF16 = jnp.bfloat16
NT_DIMS = (((1,), (1,)), ((), ()))
TN_DIMS = (((0,), (0,)), ((), ()))


def _params(sem, vmem=VMEM_LIMIT):
    return pltpu.CompilerParams(dimension_semantics=sem, vmem_limit_bytes=vmem)


def _pick_tile(n, candidates):
    for c in candidates:
        if n % c == 0:
            return c
    raise ValueError(f"no tile in {candidates} divides {n}")


def _silu(x):
    return x * jax.nn.sigmoid(x)


def _inproj_kernel(h_ref, g_ref, w_ref, ws_ref, o_ref, os_ref, u_ref):
    @pl.when(pl.program_id(1) == 0)
    def _():
        x = h_ref[...]
        ms = jnp.mean(x * x, axis=-1, keepdims=True)
        u = ((x * lax.rsqrt(ms + RMS_EPS)) * g_ref[...]).astype(BF16)
        u_ref[...] = u
        os_ref[...] = jnp.dot(u, ws_ref[...], preferred_element_type=F32)

    o_ref[...] = jnp.dot(u_ref[...], w_ref[...], preferred_element_type=F32)


def _inproj(h, g, w_main, w_small, tm, tn=512):
    lp = h.shape[0]
    n_main = w_main.shape[1]
    n_small = w_small.shape[1]
    return pl.pallas_call(
        _inproj_kernel,
        out_shape=(jax.ShapeDtypeStruct((lp, n_main), F32),
                   jax.ShapeDtypeStruct((lp, n_small), F32)),
        grid=(lp // tm, n_main // tn),
        in_specs=[
            pl.BlockSpec((tm, D_MODEL), lambda m, n: (m, 0)),
            pl.BlockSpec((1, D_MODEL), lambda m, n: (0, 0)),
            pl.BlockSpec((D_MODEL, tn), lambda m, n: (0, n)),
            pl.BlockSpec((D_MODEL, n_small), lambda m, n: (0, 0)),
        ],
        out_specs=(
            pl.BlockSpec((tm, tn), lambda m, n: (m, n)),
            pl.BlockSpec((tm, n_small), lambda m, n: (m, 0)),
        ),
        scratch_shapes=[pltpu.VMEM((tm, D_MODEL), BF16)],
        compiler_params=_params(("parallel", "arbitrary")),
        name="inproj",
    )(h, g, w_main, w_small)


def _qkprep_kernel(aq_ref, ak_ref, av_ref, gq_ref, gk_ref, q_out, k_out, v_out):
    tm = aq_ref.shape[0]
    lo = lax.broadcasted_iota(jnp.int32, (tm, LANES), 1) < A_DH
    q_scale = (A_DH ** -0.5) * LOG2E

    def norm_pair(x, g):
        ss = x * x
        s_lo = jnp.sum(jnp.where(lo, ss, 0.0), axis=-1, keepdims=True)
        s_hi = jnp.sum(jnp.where(lo, 0.0, ss), axis=-1, keepdims=True)
        ms = jnp.where(lo, s_lo, s_hi) * (1.0 / A_DH)
        return (x * lax.rsqrt(ms + RMS_EPS)) * g

    for hh in range(A_HEADS):
        sl = slice(hh * LANES, (hh + 1) * LANES)
        q_out[:, sl] = (norm_pair(aq_ref[:, sl], gq_ref[...]) * q_scale).astype(BF16)
        k_out[:, sl] = norm_pair(ak_ref[:, sl], gk_ref[...]).astype(BF16)
    v_out[...] = av_ref[...].astype(BF16)


def _qkprep(proj, gq2, gk2, tm):
    lp = proj.shape[0]
    blk = lambda j: pl.BlockSpec((tm, BLK), lambda m, j=j: (m, j))
    vec = pl.BlockSpec((1, LANES), lambda m: (0, 0))
    out = jax.ShapeDtypeStruct((lp, BLK), BF16)
    return pl.pallas_call(
        _qkprep_kernel,
        out_shape=(out, out, out),
        grid=(lp // tm,),
        in_specs=[blk(0), blk(1), blk(2), vec, vec],
        out_specs=(blk(0), blk(0), blk(0)),
        compiler_params=_params(("parallel",)),
        name="qkprep",
    )(proj, proj, proj, gq2, gk2)


def _attn_kernel(lq1_ref, lk1_ref, lq2_ref, lk2_ref, q_ref, k_ref, v_ref, g_ref, o_ref,
                 m_ref, l_ref, acc_ref, *, tq, tk_big, lam_init):
    h = pl.program_id(0)
    i = pl.program_id(1)
    slope2 = jnp.exp2(-(ALIBI_MAX_EXP / A_HEADS) * (h + 1).astype(F32)) * LOG2E
    q = q_ref[...]
    lane = lax.broadcasted_iota(jnp.int32, (tq, LANES), 1)
    zero = jnp.zeros_like(q)
    q_maps = (jnp.where(lane < A_DH, q, zero), jnp.where(lane >= A_DH, q, zero))
    m_ref[...] = jnp.full(m_ref.shape, -jnp.inf, F32)
    l_ref[...] = jnp.zeros(l_ref.shape, F32)
    acc_ref[...] = jnp.zeros(acc_ref.shape, F32)
    row0 = i * tq

    def block(start, size, masked):
        k = k_ref[pl.ds(start, size), :]
        v = v_ref[pl.ds(start, size), :]
        col = lax.broadcasted_iota(jnp.int32, (1, size), 1) + (start - row0)
        bias = slope2 * col.astype(F32)
        if masked:
            keep = col <= lax.broadcasted_iota(jnp.int32, (tq, 1), 0)
        for c in range(2):
            s = lax.dot_general(q_maps[c], k, NT_DIMS, preferred_element_type=F32) + bias
            if masked:
                s = jnp.where(keep, s, -jnp.inf)
            m_prev = m_ref[c]
            m_new = jnp.maximum(m_prev, jnp.max(s, axis=-1, keepdims=True))
            alpha = jnp.exp2(m_prev - m_new)
            p = jnp.exp2(s - m_new)
            l_ref[c] = alpha * l_ref[c] + jnp.sum(p, axis=-1, keepdims=True)
            acc_ref[c] = alpha * acc_ref[c] + jnp.dot(p.astype(BF16), v,
                                                      preferred_element_type=F32)
            m_ref[c] = m_new

    n_big = row0 // tk_big

    def big_body(j, carry):
        block(pl.multiple_of(j * tk_big, tk_big), tk_big, False)
        return carry

    lax.fori_loop(0, n_big, big_body, 0)
    base = n_big * tk_big
    n_small = (row0 - base) // tq

    def small_body(j, carry):
        block(pl.multiple_of(base + j * tq, tq), tq, False)
        return carry

    lax.fori_loop(0, n_small, small_body, 0)
    block(pl.multiple_of(row0, tq), tq, True)

    lam = (jnp.exp(jnp.sum(lq1_ref[...] * lk1_ref[...], axis=-1, keepdims=True))
           - jnp.exp(jnp.sum(lq2_ref[...] * lk2_ref[...], axis=-1, keepdims=True))
           + lam_init)
    o = acc_ref[0] / l_ref[0] - lam * (acc_ref[1] / l_ref[1])
    ms = jnp.mean(o * o, axis=-1, keepdims=True)
    y = ((o * lax.rsqrt(ms + RMS_EPS)) * g_ref[...]) * (1.0 - lam_init)
    o_ref[...] = y.astype(BF16)


def _attn(qn, kn, vb, lq1, lk1, lq2, lk2, g_sub, lam_init):
    lp = qn.shape[0]
    tq = ATTN_TQ
    lamspec = pl.BlockSpec((1, A_DH), lambda h, i: (0, 0))
    kv = pl.BlockSpec((lp, LANES), lambda h, i: (0, h))
    return pl.pallas_call(
        functools.partial(_attn_kernel, tq=tq, tk_big=ATTN_TK_BIG, lam_init=lam_init),
        out_shape=jax.ShapeDtypeStruct((lp, A_HEADS * A_DV), BF16),
        grid=(A_HEADS, lp // tq),
        in_specs=[lamspec, lamspec, lamspec, lamspec,
                  pl.BlockSpec((tq, LANES), lambda h, i: (i, h)), kv, kv,
                  pl.BlockSpec((1, A_DV), lambda h, i: (0, 0))],
        out_specs=pl.BlockSpec((tq, A_DV), lambda h, i: (i, h)),
        scratch_shapes=[pltpu.VMEM((2, tq, 1), F32), pltpu.VMEM((2, tq, 1), F32),
                        pltpu.VMEM((2, tq, A_DV), F32)],
        compiler_params=_params(("parallel", "arbitrary")),
        name="attn",
    )(lq1, lk1, lq2, lk2, qn, kn, vb, g_sub)


def _gdnprep_kernel(xq_ref, xk_ref, xv_ref, cw_ref, sm_ref, alog_ref, dtb_ref,
                    q_out, k_out, v_out, beta_out, gcum_out, gcumt_out, xbuf_ref):
    tm = xq_ref.shape[0]
    halo = SUBLANES

    @pl.when(pl.program_id(0) == 0)
    def _():
        xbuf_ref[0:halo, :] = jnp.zeros((halo, 3 * BLK), F32)

    xbuf_ref[halo:, 0:BLK] = xq_ref[...]
    xbuf_ref[halo:, BLK:2 * BLK] = xk_ref[...]
    xbuf_ref[halo:, 2 * BLK:] = xv_ref[...]

    outs = (q_out, k_out, v_out)
    for s in range(3 * BLK // LANES):
        sl = slice(s * LANES, (s + 1) * LANES)
        acc = None
        for t in range(DN_CONV):
            off = halo - (DN_CONV - 1) + t
            term = xbuf_ref[off:off + tm, sl] * cw_ref[t:t + 1, sl]
            acc = term if acc is None else acc + term
        y = _silu(acc)
        part, hh = divmod(s, DN_HEADS)
        if part < 2:
            y = y * lax.rsqrt(jnp.sum(y * y, axis=-1, keepdims=True) + L2_EPS)
            if part == 0:
                y = y * (DN_DK ** -0.5)
        outs[part][:, hh * LANES:(hh + 1) * LANES] = y.astype(BF16)

    xbuf_ref[0:halo, :] = xbuf_ref[tm:tm + halo, :]

    beta_out[...] = jax.nn.sigmoid(sm_ref[:, 0:LANES])
    g = -jnp.exp(alog_ref[...]) * jax.nn.softplus(sm_ref[:, LANES:] + dtb_ref[...])
    g1 = g.astype(BF16)
    r1 = g - g1.astype(F32)
    g2 = r1.astype(BF16)
    g3 = (r1 - g2.astype(F32)).astype(BF16)
    ii = lax.broadcasted_iota(jnp.int32, (GDN_CHUNK, GDN_CHUNK), 0)
    jj = lax.broadcasted_iota(jnp.int32, (GDN_CHUNK, GDN_CHUNK), 1)
    tril = (ii >= jj).astype(BF16)
    for cidx in range(tm // GDN_CHUNK):
        rs = slice(cidx * GDN_CHUNK, (cidx + 1) * GDN_CHUNK)
        gc = (jnp.dot(tril, g1[rs], preferred_element_type=F32)
              + jnp.dot(tril, g2[rs], preferred_element_type=F32)
              + jnp.dot(tril, g3[rs], preferred_element_type=F32))
        gcum_out[rs, :] = gc
        gcumt_out[:, rs] = gc.T[0:SUBLANES, :]


def _gdnprep(proj, small, conv_w, alog, dtb, tm):
    lp = proj.shape[0]
    blk = lambda j: pl.BlockSpec((tm, BLK), lambda m, j=j: (m, j))
    row = pl.BlockSpec((tm, LANES), lambda m: (m, 0))
    vec = pl.BlockSpec((1, LANES), lambda m: (0, 0))
    o16 = jax.ShapeDtypeStruct((lp, BLK), BF16)
    o32 = jax.ShapeDtypeStruct((lp, LANES), F32)
    return pl.pallas_call(
        _gdnprep_kernel,
        out_shape=(o16, o16, o16, o32, o32, jax.ShapeDtypeStruct((SUBLANES, lp), F32)),
        grid=(lp // tm,),
        in_specs=[blk(3), blk(4), blk(5),
                  pl.BlockSpec((DN_CONV, 3 * BLK), lambda m: (0, 0)),
                  pl.BlockSpec((tm, 2 * LANES), lambda m: (m, 0)), vec, vec],
        out_specs=(blk(0), blk(0), blk(0), row, row,
                   pl.BlockSpec((SUBLANES, tm), lambda m: (0, m))),
        scratch_shapes=[pltpu.VMEM((tm + SUBLANES, 3 * BLK), F32)],
        compiler_params=_params(("arbitrary",)),
        name="gdnprep",
    )(proj, proj, proj, conv_w, small, alog, dtb)


def _gdn_kernel(q_ref, k_ref, v_ref, beta_ref, gc_ref, gt_ref, z_ref, gn_ref, o_ref, s_ref):
    c = GDN_CHUNK

    @pl.when(pl.program_id(0) == 0)
    def _():
        s_ref[...] = jnp.zeros(s_ref.shape, F32)

    ii = lax.broadcasted_iota(jnp.int32, (c, c), 0)
    jj = lax.broadcasted_iota(jnp.int32, (c, c), 1)
    ge = ii >= jj
    gt = ii > jj
    eye = (ii == jj).astype(F32)
    blk_id = lambda t: (jnp.right_shift(ii, t), jnp.right_shift(jj, t))
    same = lambda t: blk_id(t)[0] == blk_id(t)[1]
    pair_mask = same(1)
    merge_masks = [same(t + 1) & jnp.logical_not(same(t)) for t in range(1, int(math.log2(c)))]

    def mm(a, b):
        return jnp.dot(a.astype(BF16), b.astype(BF16), preferred_element_type=F32)

    for hh in range(DN_HEADS):
        sl = slice(hh * LANES, (hh + 1) * LANES)
        q = q_ref[:, sl].astype(F32)
        k = k_ref[:, sl].astype(F32)
        v = v_ref[:, sl].astype(F32)
        beta = beta_ref[:, hh:hh + 1]
        g_col = gc_ref[:, hh:hh + 1]
        g_row = gt_ref[hh:hh + 1, :]
        g_last = gc_ref[c - 1:c, hh:hh + 1]
        decay = jnp.exp(jnp.where(ge, g_col - g_row, -jnp.inf))
        e_g = jnp.exp(g_col)
        kb = k * beta
        vb = v * beta
        kk = lax.dot_general(kb.astype(BF16), k.astype(BF16), NT_DIMS, preferred_element_type=F32)
        a = jnp.where(gt, kk * decay, 0.0)
        x = eye - jnp.where(pair_mask, a, 0.0)
        for mask in merge_masks:
            x = x - mm(x, mm(jnp.where(mask, a, 0.0), x))
        uw = mm(x, jnp.concatenate([vb, kb * e_g], axis=1))
        u = uw[:, :DN_DV]
        w = uw[:, DN_DV:]
        qk = lax.dot_general(q.astype(BF16), k.astype(BF16), NT_DIMS, preferred_element_type=F32)
        qk = jnp.where(ge, qk * decay, 0.0)
        s = s_ref[hh]
        v_new = u - mm(w, s)
        o = mm(q * e_g, s) + mm(qk, v_new)
        kd = k * jnp.exp(g_last - g_col)
        s_ref[hh] = s * jnp.exp(g_last) + lax.dot_general(
            kd.astype(BF16), v_new.astype(BF16), TN_DIMS, preferred_element_type=F32)
        ms = jnp.mean(o * o, axis=-1, keepdims=True)
        y = ((o * lax.rsqrt(ms + RMS_EPS)) * gn_ref[...]) * _silu(z_ref[:, sl])
        o_ref[:, sl] = y.astype(BF16)


def _gdn(qn, kn, vn, beta, gcum, gcum_t, proj, gn):
    lp = qn.shape[0]
    c = GDN_CHUNK
    blk = lambda j: pl.BlockSpec((c, BLK), lambda n, j=j: (n, j))
    row = pl.BlockSpec((c, LANES), lambda n: (n, 0))
    return pl.pallas_call(
        _gdn_kernel,
        out_shape=jax.ShapeDtypeStruct((lp, DN_HEADS * DN_DV), BF16),
        grid=(lp // c,),
        in_specs=[blk(0), blk(0), blk(0), row, row,
                  pl.BlockSpec((SUBLANES, c), lambda n: (0, n)),
                  blk(6), pl.BlockSpec((1, DN_DV), lambda n: (0, 0))],
        out_specs=blk(0),
        scratch_shapes=[pltpu.VMEM((DN_HEADS, DN_DK, DN_DV), F32)],
        compiler_params=_params(("arbitrary",)),
        name="gdn",
    )(qn, kn, vn, beta, gcum, gcum_t, proj, gn)


def _mix_kernel(ao_ref, do_ref, ga_ref, gb_ref, h_ref, wa_ref, wb_ref, wo_ref, o_ref):
    ya = jnp.dot(ao_ref[...], wa_ref[...], preferred_element_type=F32)
    yb = jnp.dot(do_ref[...], wb_ref[...], preferred_element_type=F32)
    mixed = jax.nn.sigmoid(ga_ref[...]) * ya + jax.nn.sigmoid(gb_ref[...]) * yb
    o_ref[...] = h_ref[...] + jnp.dot(mixed.astype(BF16), wo_ref[...],
                                      preferred_element_type=F32)


def _mix(ao, do, proj, h, wa, wb, wo, tm):
    lp = h.shape[0]
    blk = lambda j: pl.BlockSpec((tm, BLK), lambda m, j=j: (m, j))
    wspec = pl.BlockSpec((BLK, D_MODEL), lambda m: (0, 0))
    return pl.pallas_call(
        _mix_kernel,
        out_shape=jax.ShapeDtypeStruct((lp, D_MODEL), F32),
        grid=(lp // tm,),
        in_specs=[blk(0), blk(0), blk(7), blk(8), blk(0), wspec, wspec, wspec],
        out_specs=blk(0),
        compiler_params=_params(("parallel",)),
        name="mix",
    )(ao, do, proj, proj, h, wa, wb, wo)


def _ffn_kernel(h_ref, g_ref, wup_ref, cw_ref, wdn_ref, o_ref, fbuf_ref, *, tc):
    tm = h_ref.shape[0]
    halo = SUBLANES

    @pl.when(pl.program_id(0) == 0)
    def _():
        fbuf_ref[0:halo, :] = jnp.zeros((halo, 2 * D_FF), F32)

    @pl.when(pl.program_id(0) > 0)
    def _():
        fbuf_ref[0:halo, :] = fbuf_ref[tm:tm + halo, :]

    x = h_ref[...]
    ms = jnp.mean(x * x, axis=-1, keepdims=True)
    u = ((x * lax.rsqrt(ms + RMS_EPS)) * g_ref[...]).astype(BF16)
    for j in range(2 * D_FF // tc):
        sl = slice(j * tc, (j + 1) * tc)
        fbuf_ref[halo:, sl] = jnp.dot(u, wup_ref[:, sl], preferred_element_type=F32)

    def conv(sl):
        acc = None
        for t in range(FFN_CONV):
            off = halo - (FFN_CONV - 1) + t
            term = fbuf_ref[off:off + tm, sl] * cw_ref[t:t + 1, sl]
            acc = term if acc is None else acc + term
        return acc

    acc = x
    for j in range(D_FF // tc):
        gate = conv(slice(j * tc, (j + 1) * tc))
        up = conv(slice(D_FF + j * tc, D_FF + (j + 1) * tc))
        act = (_silu(gate) * up).astype(BF16)
        acc = acc + jnp.dot(act, wdn_ref[j * tc:(j + 1) * tc, :], preferred_element_type=F32)
    o_ref[...] = acc


def _ffn(h, g, wup, cw, wdn, tm, tc=256):
    lp = h.shape[0]
    once = pl.Buffered(1)
    return pl.pallas_call(
        functools.partial(_ffn_kernel, tc=tc),
        out_shape=jax.ShapeDtypeStruct((lp, D_MODEL), F32),
        grid=(lp // tm,),
        in_specs=[pl.BlockSpec((tm, D_MODEL), lambda m: (m, 0)),
                  pl.BlockSpec((1, D_MODEL), lambda m: (0, 0)),
                  pl.BlockSpec((D_MODEL, 2 * D_FF), lambda m: (0, 0), pipeline_mode=once),
                  pl.BlockSpec((FFN_CONV, 2 * D_FF), lambda m: (0, 0)),
                  pl.BlockSpec((D_FF, D_MODEL), lambda m: (0, 0), pipeline_mode=once)],
        out_specs=pl.BlockSpec((tm, D_MODEL), lambda m: (m, 0)),
        scratch_shapes=[pltpu.VMEM((tm + SUBLANES, 2 * D_FF), F32)],
        compiler_params=_params(("arbitrary",)),
        name="ffn",
    )(h, g, wup, cw, wdn)


def _regroup_w_in(w):
    a_cols = 3 * A_HEADS * A_DV
    dn_cols = DN_HEADS * (2 * DN_DK + DN_DV) + DN_HEADS * DN_DV
    o_small = a_cols + dn_cols
    w_main = jnp.concatenate([w[:, :o_small], w[:, o_small + 2 * DN_HEADS:]], axis=1)
    w_small = jnp.zeros((D_MODEL, 2 * LANES), w.dtype)
    w_small = w_small.at[:, 0:DN_HEADS].set(w[:, o_small:o_small + DN_HEADS])
    w_small = w_small.at[:, LANES:LANES + DN_HEADS].set(
        w[:, o_small + DN_HEADS:o_small + 2 * DN_HEADS])
    return w_main.astype(BF16), w_small.astype(BF16)


def _pad_lanes(v):
    return jnp.zeros((1, LANES), F32).at[0, :v.shape[0]].set(v.astype(F32))


def kernel(x, meta_tokens, mix_norm_g, w_in, q_norm_g, k_norm_g, lambda_q1, lambda_k1, lambda_q2, lambda_k2, attn_subln_g, dn_conv_w, dn_a_log, dn_dt_bias, dn_norm_g, w_branch_attn, w_branch_dn, w_out, ffn_norm_g, w_ffn_up, ffn_conv_w, w_ffn_down):
    batch, seq, _ = x.shape
    assert batch == 1
    depth = w_in.shape[0]
    length = N_META + seq
    lp = -(-length // ROW_ALIGN) * ROW_ALIGN
    tm = _pick_tile(lp, (640, 512, 256))
    tm_ffn = _pick_tile(lp, (320, 256))
    h = jnp.concatenate([meta_tokens.astype(F32), x[0],
                         jnp.zeros((lp - length, D_MODEL), F32)], axis=0)
    row = lambda v: v.astype(F32)[None, :]
    for layer in range(depth):
        lam_init = 0.8 - 0.6 * math.exp(-0.3 * layer)
        w_main, w_small = _regroup_w_in(w_in[layer])
        proj, small = _inproj(h, row(mix_norm_g[layer]), w_main, w_small, tm)

        gq2 = jnp.tile(row(q_norm_g[layer]), (1, 2))
        gk2 = jnp.tile(row(k_norm_g[layer]), (1, 2))
        qn, kn, vb = _qkprep(proj, gq2, gk2, tm)
        ao = _attn(qn, kn, vb, row(lambda_q1[layer]), row(lambda_k1[layer]),
                   row(lambda_q2[layer]), row(lambda_k2[layer]), row(attn_subln_g[layer]),
                   lam_init)

        dq, dk, dv, beta, gcum, gcum_t = _gdnprep(
            proj, small, dn_conv_w[layer].astype(F32), _pad_lanes(dn_a_log[layer]),
            _pad_lanes(dn_dt_bias[layer]), tm)
        do = _gdn(dq, dk, dv, beta, gcum, gcum_t, proj, row(dn_norm_g[layer]))

        h = _mix(ao, do, proj, h, w_branch_attn[layer].astype(BF16),
                 w_branch_dn[layer].astype(BF16), w_out[layer].astype(BF16), tm)
        h = _ffn(h, row(ffn_norm_g[layer]), w_ffn_up[layer].astype(BF16),
                 ffn_conv_w[layer].astype(F32), w_ffn_down[layer].astype(BF16), tm_ffn)
    return h[N_META:N_META + seq][None]
```

```python
import functools
import math

import jax
import jax.numpy as jnp
import ml_dtypes
import numpy as np
from jax import lax
from jax.experimental import pallas as pl
from jax.experimental.pallas import tpu as pltpu

D_MODEL = 1024
N_META = 16
A_HEADS = 8
A_DH = 64
A_DV = 2 * A_DH
DN_HEADS = 8
DN_DK = 128
DN_DV = 128
DN_CONV = 4
D_FF = 2816
FFN_CONV = 3
RMS_EPS = 1e-6
L2_EPS = 1e-6
ALIBI_MAX_EXP = 8.0

LANES = 128
SUBLANES = 8
ROW_ALIGN = 256
ATTN_TQ = 256
ATTN_TK_BIG = 1024
GDN_CHUNK = 128
BLK = 1024
LOG2E = 1.4426950408889634
VMEM_LIMIT = 52 * 1024 * 1024

FAST_EXP_BOUND = 60.0
ZERO_SCALE_EXP = 150.0

AUG_ROW = 0
AUG_JR = 3
AUG_A4 = 6
AUG_JREL = 9
N_PIECES = 3
AUG_BASE = (A_DH, 0)

F32 = jnp.float32
BF16 = jnp.bfloat16
NT_DIMS = (((1,), (1,)), ((), ()))
TN_DIMS = (((0,), (0,)), ((), ()))


def _params(sem, vmem=VMEM_LIMIT):
    return pltpu.CompilerParams(dimension_semantics=sem, vmem_limit_bytes=vmem)


def _pick_tile(n, candidates):
    for c in candidates:
        if n % c == 0:
            return c
    raise ValueError(f"no tile in {candidates} divides {n}")


def _silu(x):
    return x * jax.nn.sigmoid(x)


def _bf16_pieces(x):
    out, r = [], np.float64(x)
    for _ in range(N_PIECES):
        p = float(np.float32(r).astype(ml_dtypes.bfloat16).astype(np.float32))
        out.append(p)
        r -= p
    return out


def _alibi_tables():
    qtab = np.zeros((A_HEADS, SUBLANES, LANES), np.float32)
    ktab = np.zeros((2 * A_HEADS, LANES), np.float32)
    for h in range(A_HEADS):
        slope = 2.0 ** (-(ALIBI_MAX_EXP / A_HEADS) * (h + 1))
        pieces = _bf16_pieces(slope * LOG2E)
        qtab[h, 6, :] = np.float32(sum(pieces))
        for c in range(2):
            b = AUG_BASE[c]
            for i, p in enumerate(pieces):
                ktab[2 * h + c, b + AUG_ROW + i] = -p
                for variant in (0, 1):
                    qtab[h, 3 * c + variant, b + AUG_JR + i] = -p
                qtab[h, 3 * c + 0, b + AUG_A4 + i] = -256.0 * p
                qtab[h, 3 * c + 2, b + AUG_JREL + i] = p
    return jnp.asarray(qtab), jnp.asarray(ktab)


def _inproj_kernel(h_ref, g_ref, w_ref, ws_ref, o_ref, os_ref, u_ref):
    @pl.when(pl.program_id(1) == 0)
    def _():
        x = h_ref[...]
        ms = jnp.mean(x * x, axis=-1, keepdims=True)
        u = ((x * lax.rsqrt(ms + RMS_EPS)) * g_ref[...]).astype(BF16)
        u_ref[...] = u
        os_ref[...] = jnp.dot(u, ws_ref[...], preferred_element_type=F32)

    o_ref[...] = jnp.dot(u_ref[...], w_ref[...], preferred_element_type=F32).astype(o_ref.dtype)


def _inproj(h, g, w_main, w_small, tm, tn=BLK):
    lp = h.shape[0]
    n_main = w_main.shape[1]
    n_small = w_small.shape[1]
    return pl.pallas_call(
        _inproj_kernel,
        out_shape=(jax.ShapeDtypeStruct((lp, n_main), BF16),
                   jax.ShapeDtypeStruct((lp, n_small), F32)),
        grid=(lp // tm, n_main // tn),
        in_specs=[
            pl.BlockSpec((tm, D_MODEL), lambda m, n: (m, 0)),
            pl.BlockSpec((1, D_MODEL), lambda m, n: (0, 0)),
            pl.BlockSpec((D_MODEL, tn), lambda m, n: (0, n)),
            pl.BlockSpec((D_MODEL, n_small), lambda m, n: (0, 0)),
        ],
        out_specs=(
            pl.BlockSpec((tm, tn), lambda m, n: (m, n)),
            pl.BlockSpec((tm, n_small), lambda m, n: (m, 0)),
        ),
        scratch_shapes=[pltpu.VMEM((tm, D_MODEL), BF16)],
        compiler_params=_params(("parallel", "arbitrary")),
        name="inproj",
    )(h, g, w_main, w_small)


def _qkprep_kernel(aq_ref, ak_ref, av_ref, gq_ref, gk_ref, ktab_ref, q_out, k_out, v_out, stat_out):
    tm = aq_ref.shape[0]
    lane = lax.broadcasted_iota(jnp.int32, (tm, LANES), 1)
    lo = lane < A_DH
    q_scale = (A_DH ** -0.5) * LOG2E

    def norm_pair(x, g):
        ss = x * x
        s_lo = jnp.sum(jnp.where(lo, ss, 0.0), axis=-1, keepdims=True)
        s_hi = jnp.sum(jnp.where(lo, 0.0, ss), axis=-1, keepdims=True)
        ms = jnp.where(lo, s_lo, s_hi) * (1.0 / A_DH)
        return (x * lax.rsqrt(ms + RMS_EPS)) * g

    def max_sq_norms(y):
        yy = y * y
        n_lo = jnp.sum(jnp.where(lo, yy, 0.0), axis=-1, keepdims=True)
        n_hi = jnp.sum(jnp.where(lo, 0.0, yy), axis=-1, keepdims=True)
        return jnp.max(n_lo, axis=0, keepdims=True), jnp.max(n_hi, axis=0, keepdims=True)

    j = (pl.program_id(0) * tm + lax.broadcasted_iota(jnp.int32, (tm, LANES), 0))
    jrel = jnp.bitwise_and(j, ATTN_TQ - 1)
    jr = (ATTN_TQ - 1) - jrel
    a4 = (ATTN_TK_BIG // ATTN_TQ - 1) - jnp.bitwise_and(
        jnp.right_shift(j, int(math.log2(ATTN_TQ))), ATTN_TK_BIG // ATTN_TQ - 1)
    pos = []
    for c in range(2):
        la = lane - AUG_BASE[c]
        pick = lambda off, la=la: (la >= off) & (la < off + N_PIECES)
        pos.append(jnp.where(pick(AUG_JR), jr, jnp.where(pick(AUG_A4), a4,
                   jnp.where(pick(AUG_JREL), jrel, 0))).astype(F32))

    lane1 = lax.broadcasted_iota(jnp.int32, (1, LANES), 1)
    stat_q = jnp.zeros((1, LANES), F32)
    stat_k = jnp.zeros((1, LANES), F32)
    for hh in range(A_HEADS):
        sl = slice(hh * LANES, (hh + 1) * LANES)
        yq = norm_pair(aq_ref[:, sl].astype(F32), gq_ref[...]) * q_scale
        yk = norm_pair(ak_ref[:, sl].astype(F32), gk_ref[...])
        q_out[:, sl] = yq.astype(BF16)
        k_out[:, 2 * hh * LANES:(2 * hh + 1) * LANES] = jnp.where(
            lo, yk, pos[0] + ktab_ref[2 * hh:2 * hh + 1, :]).astype(BF16)
        k_out[:, (2 * hh + 1) * LANES:(2 * hh + 2) * LANES] = jnp.where(
            lo, pos[1] + ktab_ref[2 * hh + 1:2 * hh + 2, :], yk).astype(BF16)
        for c, (mq, mk) in enumerate(zip(max_sq_norms(yq), max_sq_norms(yk))):
            stat_q = jnp.maximum(stat_q, jnp.where(lane1 == 2 * hh + c, mq, 0.0))
            stat_k = jnp.maximum(stat_k, jnp.where(lane1 == 2 * hh + c, mk, 0.0))
    v_out[...] = av_ref[...]

    @pl.when(pl.program_id(0) == 0)
    def _():
        stat_out[...] = jnp.zeros(stat_out.shape, F32)

    stat_out[0:1, :] = jnp.maximum(stat_out[0:1, :], stat_q)
    stat_out[1:2, :] = jnp.maximum(stat_out[1:2, :], stat_k)


def _qkprep(proj, gq2, gk2, ktab, tm):
    lp = proj.shape[0]
    blk = lambda j: pl.BlockSpec((tm, BLK), lambda m, j=j: (m, j))
    vec = pl.BlockSpec((1, LANES), lambda m: (0, 0))
    out = jax.ShapeDtypeStruct((lp, BLK), BF16)
    return pl.pallas_call(
        _qkprep_kernel,
        out_shape=(out, jax.ShapeDtypeStruct((lp, 2 * BLK), BF16), out,
                   jax.ShapeDtypeStruct((SUBLANES, LANES), F32)),
        grid=(lp // tm,),
        in_specs=[blk(0), blk(1), blk(2), vec, vec,
                  pl.BlockSpec((2 * A_HEADS, LANES), lambda m: (0, 0))],
        out_specs=(blk(0), pl.BlockSpec((tm, 2 * BLK), lambda m: (m, 0)), blk(0),
                   pl.BlockSpec((SUBLANES, LANES), lambda m: (0, 0))),
        compiler_params=_params(("arbitrary",)),
        name="qkprep",
    )(proj, proj, proj, gq2, gk2, ktab)


def _attn_finish(lq1_ref, lk1_ref, lq2_ref, lk2_ref, g_ref, o_ref, o1, o2, lam_init):
    lam = (jnp.exp(jnp.sum(lq1_ref[...] * lk1_ref[...], axis=-1, keepdims=True))
           - jnp.exp(jnp.sum(lq2_ref[...] * lk2_ref[...], axis=-1, keepdims=True))
           + lam_init)
    o = o1 - lam * o2
    ms = jnp.mean(o * o, axis=-1, keepdims=True)
    y = ((o * lax.rsqrt(ms + RMS_EPS)) * g_ref[...]) * (1.0 - lam_init)
    o_ref[...] = y.astype(BF16)


def _attn_fast_kernel(lq1_ref, lk1_ref, lq2_ref, lk2_ref, qtab_ref, q_ref, k_ref, v_ref, g_ref,
                      o_ref, l_ref, acc_ref, *, lam_init):
    tq, tk, tb = ATTN_TQ, ATTN_TQ, ATTN_TK_BIG
    h = pl.program_id(0)
    i = pl.program_id(1)
    row0 = i * tq
    tab = qtab_ref[0]
    slope2 = tab[6:7, :]
    q = q_ref[...].astype(F32)
    lane = lax.broadcasted_iota(jnp.int32, (tq, LANES), 1)
    rowi = lax.broadcasted_iota(jnp.int32, (tq, LANES), 0).astype(F32)
    qa = [[None, None] for _ in range(3)]
    for c in range(2):
        content = (lane < A_DH) if c == 0 else (lane >= A_DH)
        la = lane - AUG_BASE[c]
        rows = jnp.where((la >= AUG_ROW) & (la < AUG_ROW + N_PIECES), rowi, 0.0)
        for variant in range(3):
            const = tab[3 * c + variant:3 * c + variant + 1, :]
            qa[variant][c] = jnp.where(content, q, rows + const).astype(BF16)
    l_ref[...] = jnp.zeros(l_ref.shape, F32)
    acc_ref[...] = jnp.zeros(acc_ref.shape, F32)
    keep = (lax.broadcasted_iota(jnp.int32, (1, tk), 1)
            <= lax.broadcasted_iota(jnp.int32, (tq, 1), 0))

    def block(start, n_tiles, variant, scale_dist):
        for c in range(2):
            pv = None
            lsum = None
            for t in range(n_tiles):
                rows = pl.ds(start + t * tk, tk)
                s = lax.dot_general(qa[variant][c], k_ref[rows, c * LANES:(c + 1) * LANES],
                                    NT_DIMS, preferred_element_type=F32)
                if scale_dist is None:
                    s = jnp.where(keep, s, -jnp.inf)
                p = jnp.exp2(s)
                part = p[:, :LANES] + p[:, LANES:]
                lsum = part if lsum is None else lsum + part
                d = jnp.dot(p.astype(BF16), v_ref[rows, :], preferred_element_type=F32)
                pv = d if pv is None else pv + d
            if scale_dist is not None:
                scale = jnp.exp2(-slope2 * scale_dist.astype(F32))
                pv = pv * scale
                lsum = lsum * scale
            acc_ref[c] += pv
            l_ref[c] += lsum

    n_big = row0 // tb
    assert ALIBI_MAX_EXP == A_HEADS
    zero_dist = jnp.left_shift(jnp.int32(math.ceil(ZERO_SCALE_EXP / LOG2E * 2.0)), h)
    first_big = jnp.right_shift(jnp.maximum(row0 - zero_dist, 0), int(math.log2(tb)))

    def big_body(jb, carry):
        start = pl.multiple_of(jb * tb, tb)
        block(start, tb // tk, 0, row0 - (start + tb - 1))
        return carry

    lax.fori_loop(first_big, n_big, big_body, 0)
    base = n_big * tb
    n_small = (row0 - base) // tk

    def small_body(js, carry):
        start = pl.multiple_of(base + js * tk, tk)
        block(start, 1, 1, row0 - (start + tk - 1))
        return carry

    lax.fori_loop(0, n_small, small_body, 0)
    block(pl.multiple_of(row0, tq), 1, 2, None)

    o1 = acc_ref[0] / jnp.sum(l_ref[0], axis=-1, keepdims=True)
    o2 = acc_ref[1] / jnp.sum(l_ref[1], axis=-1, keepdims=True)
    _attn_finish(lq1_ref, lk1_ref, lq2_ref, lk2_ref, g_ref, o_ref, o1, o2, lam_init)


def _attn_safe_kernel(lq1_ref, lk1_ref, lq2_ref, lk2_ref, q_ref, k_ref, v_ref, g_ref, o_ref,
                      m_ref, l_ref, acc_ref, *, lam_init):
    tq, tb = ATTN_TQ, ATTN_TK_BIG
    h = pl.program_id(0)
    i = pl.program_id(1)
    slope2 = jnp.exp2(-(ALIBI_MAX_EXP / A_HEADS) * (h + 1).astype(F32)) * LOG2E
    q = q_ref[...].astype(F32)
    lane = lax.broadcasted_iota(jnp.int32, (tq, LANES), 1)
    q_maps = (jnp.where(lane < A_DH, q, 0.0).astype(BF16), jnp.where(lane >= A_DH, q, 0.0).astype(BF16))
    m_ref[...] = jnp.full(m_ref.shape, -jnp.inf, F32)
    l_ref[...] = jnp.zeros(l_ref.shape, F32)
    acc_ref[...] = jnp.zeros(acc_ref.shape, F32)
    row0 = i * tq

    def block(start, size, masked):
        v = v_ref[pl.ds(start, size), :]
        col = lax.broadcasted_iota(jnp.int32, (1, size), 1) + (start - row0)
        bias = slope2 * col.astype(F32)
        if masked:
            keep = col <= lax.broadcasted_iota(jnp.int32, (tq, 1), 0)
        for c in range(2):
            k = k_ref[pl.ds(start, size), c * LANES:(c + 1) * LANES]
            s = lax.dot_general(q_maps[c], k, NT_DIMS, preferred_element_type=F32) + bias
            if masked:
                s = jnp.where(keep, s, -jnp.inf)
            m_prev = m_ref[c]
            m_new = jnp.maximum(m_prev, jnp.max(s, axis=-1, keepdims=True))
            alpha = jnp.exp2(m_prev - m_new)
            p = jnp.exp2(s - m_new)
            l_ref[c] = alpha * l_ref[c] + jnp.sum(p, axis=-1, keepdims=True)
            acc_ref[c] = alpha * acc_ref[c] + jnp.dot(p.astype(BF16), v,
                                                      preferred_element_type=F32)
            m_ref[c] = m_new

    n_big = row0 // tb

    def big_body(jb, carry):
        block(pl.multiple_of(jb * tb, tb), tb, False)
        return carry

    lax.fori_loop(0, n_big, big_body, 0)
    base = n_big * tb
    n_small = (row0 - base) // tq

    def small_body(js, carry):
        block(pl.multiple_of(base + js * tq, tq), tq, False)
        return carry

    lax.fori_loop(0, n_small, small_body, 0)
    block(pl.multiple_of(row0, tq), tq, True)
    _attn_finish(lq1_ref, lk1_ref, lq2_ref, lk2_ref, g_ref, o_ref,
                 acc_ref[0] / l_ref[0], acc_ref[1] / l_ref[1], lam_init)


def _attn(qn, ka, vb, stat, qtab, lq1, lk1, lq2, lk2, g_sub, lam_init):
    lp = qn.shape[0]
    tq = ATTN_TQ
    lamspec = pl.BlockSpec((1, A_DH), lambda h, i: (0, 0))
    common = dict(
        out_shape=jax.ShapeDtypeStruct((lp, A_HEADS * A_DV), BF16),
        grid=(A_HEADS, lp // tq),
        out_specs=pl.BlockSpec((tq, A_DV), lambda h, i: (i, h)),
        compiler_params=_params(("parallel", "arbitrary")),
    )
    data_specs = [pl.BlockSpec((tq, LANES), lambda h, i: (i, h)),
                  pl.BlockSpec((lp, 2 * LANES), lambda h, i: (0, h)),
                  pl.BlockSpec((lp, LANES), lambda h, i: (0, h)),
                  pl.BlockSpec((1, A_DV), lambda h, i: (0, 0))]
    lams = [lamspec] * 4

    def fast(qn, ka, vb):
        return pl.pallas_call(
            functools.partial(_attn_fast_kernel, lam_init=lam_init),
            in_specs=lams + [pl.BlockSpec((1, SUBLANES, LANES), lambda h, i: (h, 0, 0))] + data_specs,
            scratch_shapes=[pltpu.VMEM((2, tq, LANES), F32), pltpu.VMEM((2, tq, A_DV), F32)],
            name="attn_fast", **common,
        )(lq1, lk1, lq2, lk2, qtab, qn, ka, vb, g_sub)

    def safe(qn, ka, vb):
        return pl.pallas_call(
            functools.partial(_attn_safe_kernel, lam_init=lam_init),
            in_specs=lams + data_specs,
            scratch_shapes=[pltpu.VMEM((2, tq, 1), F32), pltpu.VMEM((2, tq, 1), F32),
                            pltpu.VMEM((2, tq, A_DV), F32)],
            name="attn_safe", **common,
        )(lq1, lk1, lq2, lk2, qn, ka, vb, g_sub)

    bound = jnp.sqrt(jnp.max(stat[0] * stat[1]))
    return lax.cond(bound <= FAST_EXP_BOUND, fast, safe, qn, ka, vb)


def _gdnprep_kernel(xq_ref, xk_ref, xv_ref, cw_ref, sm_ref, alog_ref, dtb_ref,
                    q_out, k_out, v_out, beta_out, gcum_out, gcumt_out, xbuf_ref):
    tm = xq_ref.shape[0]
    halo = SUBLANES

    @pl.when(pl.program_id(0) == 0)
    def _():
        xbuf_ref[0:halo, :] = jnp.zeros((halo, 3 * BLK), F32)

    xbuf_ref[halo:, 0:BLK] = xq_ref[...].astype(F32)
    xbuf_ref[halo:, BLK:2 * BLK] = xk_ref[...].astype(F32)
    xbuf_ref[halo:, 2 * BLK:] = xv_ref[...].astype(F32)

    outs = (q_out, k_out, v_out)
    for s in range(3 * BLK // LANES):
        sl = slice(s * LANES, (s + 1) * LANES)
        acc = None
        for t in range(DN_CONV):
            off = halo - (DN_CONV - 1) + t
            term = xbuf_ref[off:off + tm, sl] * cw_ref[t:t + 1, sl]
            acc = term if acc is None else acc + term
        y = _silu(acc)
        part, hh = divmod(s, DN_HEADS)
        if part < 2:
            y = y * lax.rsqrt(jnp.sum(y * y, axis=-1, keepdims=True) + L2_EPS)
            if part == 0:
                y = y * (DN_DK ** -0.5)
        outs[part][:, hh * LANES:(hh + 1) * LANES] = y.astype(BF16)

    xbuf_ref[0:halo, :] = xbuf_ref[tm:tm + halo, :]

    beta_out[...] = jax.nn.sigmoid(sm_ref[:, 0:LANES])
    g = -jnp.exp(alog_ref[...]) * jax.nn.softplus(sm_ref[:, LANES:] + dtb_ref[...])
    g1 = g.astype(BF16)
    r1 = g - g1.astype(F32)
    g2 = r1.astype(BF16)
    g3 = (r1 - g2.astype(F32)).astype(BF16)
    ii = lax.broadcasted_iota(jnp.int32, (GDN_CHUNK, GDN_CHUNK), 0)
    jj = lax.broadcasted_iota(jnp.int32, (GDN_CHUNK, GDN_CHUNK), 1)
    tril = (ii >= jj).astype(BF16)
    for cidx in range(tm // GDN_CHUNK):
        rs = slice(cidx * GDN_CHUNK, (cidx + 1) * GDN_CHUNK)
        gc = (jnp.dot(tril, g1[rs], preferred_element_type=F32)
              + jnp.dot(tril, g2[rs], preferred_element_type=F32)
              + jnp.dot(tril, g3[rs], preferred_element_type=F32))
        gcum_out[rs, :] = gc
        gcumt_out[:, rs] = gc.T[0:SUBLANES, :]


def _gdnprep(proj, small, conv_w, alog, dtb, tm):
    lp = proj.shape[0]
    blk = lambda j: pl.BlockSpec((tm, BLK), lambda m, j=j: (m, j))
    row = pl.BlockSpec((tm, LANES), lambda m: (m, 0))
    vec = pl.BlockSpec((1, LANES), lambda m: (0, 0))
    o16 = jax.ShapeDtypeStruct((lp, BLK), BF16)
    o32 = jax.ShapeDtypeStruct((lp, LANES), F32)
    return pl.pallas_call(
        _gdnprep_kernel,
        out_shape=(o16, o16, o16, o32, o32, jax.ShapeDtypeStruct((SUBLANES, lp), F32)),
        grid=(lp // tm,),
        in_specs=[blk(3), blk(4), blk(5),
                  pl.BlockSpec((DN_CONV, 3 * BLK), lambda m: (0, 0)),
                  pl.BlockSpec((tm, 2 * LANES), lambda m: (m, 0)), vec, vec],
        out_specs=(blk(0), blk(0), blk(0), row, row,
                   pl.BlockSpec((SUBLANES, tm), lambda m: (0, m))),
        scratch_shapes=[pltpu.VMEM((tm + SUBLANES, 3 * BLK), F32)],
        compiler_params=_params(("arbitrary",)),
        name="gdnprep",
    )(proj, proj, proj, conv_w, small, alog, dtb)


def _gdn_kernel(q_ref, k_ref, v_ref, beta_ref, gc_ref, gt_ref, z_ref, gn_ref, o_ref, s_ref):
    c = GDN_CHUNK
    heads = range(DN_HEADS)

    @pl.when(pl.program_id(0) == 0)
    def _():
        s_ref[...] = jnp.zeros(s_ref.shape, F32)

    ii = lax.broadcasted_iota(jnp.int32, (c, c), 0)
    jj = lax.broadcasted_iota(jnp.int32, (c, c), 1)
    ge = ii >= jj
    gt = ii > jj
    eye = (ii == jj).astype(F32)
    blk_id = lambda t: (jnp.right_shift(ii, t), jnp.right_shift(jj, t))
    same = lambda t: blk_id(t)[0] == blk_id(t)[1]
    pair_mask = same(1)
    merge_masks = [same(t + 1) & jnp.logical_not(same(t)) for t in range(1, int(math.log2(c)))]

    def mm(a, b):
        return jnp.dot(a.astype(BF16), b.astype(BF16), preferred_element_type=F32)

    sl = [slice(hh * LANES, (hh + 1) * LANES) for hh in heads]
    q16 = [q_ref[:, sl[hh]] for hh in heads]
    k16 = [k_ref[:, sl[hh]] for hh in heads]
    k = [x.astype(F32) for x in k16]
    beta = [beta_ref[:, hh:hh + 1] for hh in heads]
    g_col = [gc_ref[:, hh:hh + 1] for hh in heads]
    g_last = [gc_ref[c - 1:c, hh:hh + 1] for hh in heads]
    decay = [jnp.exp(jnp.where(ge, g_col[hh] - gt_ref[hh:hh + 1, :], -jnp.inf)) for hh in heads]
    e_g = [jnp.exp(g) for g in g_col]
    kb = [k[hh] * beta[hh] for hh in heads]
    vb = [v_ref[:, sl[hh]].astype(F32) * beta[hh] for hh in heads]
    kq = [lax.dot_general(jnp.concatenate([kb[hh].astype(BF16), q16[hh]], axis=0), k16[hh],
                          NT_DIMS, preferred_element_type=F32) for hh in heads]
    a = [jnp.where(gt, kq[hh][:c] * decay[hh], 0.0) for hh in heads]
    qk = [jnp.where(ge, kq[hh][c:] * decay[hh], 0.0) for hh in heads]
    x = [eye - jnp.where(pair_mask, a[hh], 0.0) for hh in heads]
    for mask in merge_masks:
        y = [mm(jnp.where(mask, a[hh], 0.0), x[hh]) for hh in heads]
        x = [x[hh] - mm(x[hh], y[hh]) for hh in heads]
    uw = [mm(x[hh], jnp.concatenate([vb[hh], kb[hh] * e_g[hh]], axis=1)) for hh in heads]
    s_old = [s_ref[hh] for hh in heads]
    ws = [mm(jnp.concatenate([uw[hh][:, DN_DV:], q16[hh].astype(F32) * e_g[hh]], axis=0), s_old[hh])
          for hh in heads]
    v_new = [uw[hh][:, :DN_DV] - ws[hh][:c] for hh in heads]
    o = [ws[hh][c:] + mm(qk[hh], v_new[hh]) for hh in heads]
    for hh in heads:
        kd = k[hh] * jnp.exp(g_last[hh] - g_col[hh])
        s_ref[hh] = s_old[hh] * jnp.exp(g_last[hh]) + lax.dot_general(
            kd.astype(BF16), v_new[hh].astype(BF16), TN_DIMS, preferred_element_type=F32)
    for hh in heads:
        ms = jnp.mean(o[hh] * o[hh], axis=-1, keepdims=True)
        y = ((o[hh] * lax.rsqrt(ms + RMS_EPS)) * gn_ref[...]) * _silu(z_ref[:, sl[hh]].astype(F32))
        o_ref[:, sl[hh]] = y.astype(BF16)


def _gdn(qn, kn, vn, beta, gcum, gcum_t, proj, gn):
    lp = qn.shape[0]
    c = GDN_CHUNK
    blk = lambda j: pl.BlockSpec((c, BLK), lambda n, j=j: (n, j))
    row = pl.BlockSpec((c, LANES), lambda n: (n, 0))
    return pl.pallas_call(
        _gdn_kernel,
        out_shape=jax.ShapeDtypeStruct((lp, DN_HEADS * DN_DV), BF16),
        grid=(lp // c,),
        in_specs=[blk(0), blk(0), blk(0), row, row,
                  pl.BlockSpec((SUBLANES, c), lambda n: (0, n)),
                  blk(6), pl.BlockSpec((1, DN_DV), lambda n: (0, 0))],
        out_specs=blk(0),
        scratch_shapes=[pltpu.VMEM((DN_HEADS, DN_DK, DN_DV), F32)],
        compiler_params=_params(("arbitrary",)),
        name="gdn",
    )(qn, kn, vn, beta, gcum, gcum_t, proj, gn)


def _mix_kernel(ao_ref, do_ref, ga_ref, gb_ref, h_ref, wa_ref, wb_ref, wo_ref, o_ref):
    ya = jnp.dot(ao_ref[...], wa_ref[...], preferred_element_type=F32)
    yb = jnp.dot(do_ref[...], wb_ref[...], preferred_element_type=F32)
    mixed = (jax.nn.sigmoid(ga_ref[...].astype(F32)) * ya
             + jax.nn.sigmoid(gb_ref[...].astype(F32)) * yb)
    o_ref[...] = h_ref[...] + jnp.dot(mixed.astype(BF16), wo_ref[...],
                                      preferred_element_type=F32)


def _mix(ao, do, proj, h, wa, wb, wo, tm):
    lp = h.shape[0]
    blk = lambda j: pl.BlockSpec((tm, BLK), lambda m, j=j: (m, j))
    wspec = pl.BlockSpec((BLK, D_MODEL), lambda m: (0, 0))
    return pl.pallas_call(
        _mix_kernel,
        out_shape=jax.ShapeDtypeStruct((lp, D_MODEL), F32),
        grid=(lp // tm,),
        in_specs=[blk(0), blk(0), blk(7), blk(8), blk(0), wspec, wspec, wspec],
        out_specs=blk(0),
        compiler_params=_params(("parallel",)),
        name="mix",
    )(ao, do, proj, proj, h, wa, wb, wo)


def _ffn_kernel(h_ref, g_ref, wup_ref, cw_ref, wdn_ref, o_ref, fbuf_ref, *, tc):
    tm = h_ref.shape[0]
    halo = SUBLANES

    @pl.when(pl.program_id(0) == 0)
    def _():
        fbuf_ref[0:halo, :] = jnp.zeros((halo, 2 * D_FF), F32)

    @pl.when(pl.program_id(0) > 0)
    def _():
        fbuf_ref[0:halo, :] = fbuf_ref[tm:tm + halo, :]

    x = h_ref[...]
    ms = jnp.mean(x * x, axis=-1, keepdims=True)
    u = ((x * lax.rsqrt(ms + RMS_EPS)) * g_ref[...]).astype(BF16)
    for j in range(2 * D_FF // tc):
        sl = slice(j * tc, (j + 1) * tc)
        fbuf_ref[halo:, sl] = jnp.dot(u, wup_ref[:, sl], preferred_element_type=F32)

    def conv(sl):
        acc = None
        for t in range(FFN_CONV):
            off = halo - (FFN_CONV - 1) + t
            term = fbuf_ref[off:off + tm, sl] * cw_ref[t:t + 1, sl]
            acc = term if acc is None else acc + term
        return acc

    acc = x
    for j in range(D_FF // tc):
        gate = conv(slice(j * tc, (j + 1) * tc))
        up = conv(slice(D_FF + j * tc, D_FF + (j + 1) * tc))
        act = (_silu(gate) * up).astype(BF16)
        acc = acc + jnp.dot(act, wdn_ref[j * tc:(j + 1) * tc, :], preferred_element_type=F32)
    o_ref[...] = acc


def _ffn(h, g, wup, cw, wdn, tm, tc=256):
    lp = h.shape[0]
    once = pl.Buffered(1)
    return pl.pallas_call(
        functools.partial(_ffn_kernel, tc=tc),
        out_shape=jax.ShapeDtypeStruct((lp, D_MODEL), F32),
        grid=(lp // tm,),
        in_specs=[pl.BlockSpec((tm, D_MODEL), lambda m: (m, 0)),
                  pl.BlockSpec((1, D_MODEL), lambda m: (0, 0)),
                  pl.BlockSpec((D_MODEL, 2 * D_FF), lambda m: (0, 0), pipeline_mode=once),
                  pl.BlockSpec((FFN_CONV, 2 * D_FF), lambda m: (0, 0)),
                  pl.BlockSpec((D_FF, D_MODEL), lambda m: (0, 0), pipeline_mode=once)],
        out_specs=pl.BlockSpec((tm, D_MODEL), lambda m: (m, 0)),
        scratch_shapes=[pltpu.VMEM((tm + SUBLANES, 2 * D_FF), F32)],
        compiler_params=_params(("arbitrary",)),
        name="ffn",
    )(h, g, wup, cw, wdn)


def _regroup_w_in(w):
    a_cols = 3 * A_HEADS * A_DV
    dn_cols = DN_HEADS * (2 * DN_DK + DN_DV) + DN_HEADS * DN_DV
    o_small = a_cols + dn_cols
    w_main = jnp.concatenate([w[:, :o_small], w[:, o_small + 2 * DN_HEADS:]], axis=1)
    w_small = jnp.zeros((D_MODEL, 2 * LANES), w.dtype)
    w_small = w_small.at[:, 0:DN_HEADS].set(w[:, o_small:o_small + DN_HEADS])
    w_small = w_small.at[:, LANES:LANES + DN_HEADS].set(
        w[:, o_small + DN_HEADS:o_small + 2 * DN_HEADS])
    return w_main.astype(BF16), w_small.astype(BF16)


def _pad_lanes(v):
    return jnp.zeros((1, LANES), F32).at[0, :v.shape[0]].set(v.astype(F32))


def kernel(x, meta_tokens, mix_norm_g, w_in, q_norm_g, k_norm_g, lambda_q1, lambda_k1, lambda_q2, lambda_k2, attn_subln_g, dn_conv_w, dn_a_log, dn_dt_bias, dn_norm_g, w_branch_attn, w_branch_dn, w_out, ffn_norm_g, w_ffn_up, ffn_conv_w, w_ffn_down):
    batch, seq, _ = x.shape
    assert batch == 1
    depth = w_in.shape[0]
    length = N_META + seq
    lp = -(-length // ROW_ALIGN) * ROW_ALIGN
    tm = _pick_tile(lp, (640, 512, 256))
    tm_ffn = _pick_tile(lp, (320, 256))
    qtab, ktab = _alibi_tables()
    h = jnp.concatenate([meta_tokens.astype(F32), x[0],
                         jnp.zeros((lp - length, D_MODEL), F32)], axis=0)
    row = lambda v: v.astype(F32)[None, :]
    for layer in range(depth):
        lam_init = 0.8 - 0.6 * math.exp(-0.3 * layer)
        w_main, w_small = _regroup_w_in(w_in[layer])
        proj, small = _inproj(h, row(mix_norm_g[layer]), w_main, w_small, tm)

        gq2 = jnp.tile(row(q_norm_g[layer]), (1, 2))
        gk2 = jnp.tile(row(k_norm_g[layer]), (1, 2))
        qn, ka, vb, stat = _qkprep(proj, gq2, gk2, ktab, tm)
        ao = _attn(qn, ka, vb, stat, qtab, row(lambda_q1[layer]), row(lambda_k1[layer]),
                   row(lambda_q2[layer]), row(lambda_k2[layer]), row(attn_subln_g[layer]),
                   lam_init)

        dq, dk, dv, beta, gcum, gcum_t = _gdnprep(
            proj, small, dn_conv_w[layer].astype(F32), _pad_lanes(dn_a_log[layer]),
            _pad_lanes(dn_dt_bias[layer]), tm)
        do = _gdn(dq, dk, dv, beta, gcum, gcum_t, proj, row(dn_norm_g[layer]))

        h = _mix(ao, do, proj, h, w_branch_attn[layer].astype(BF16),
                 w_branch_dn[layer].astype(BF16), w_out[layer].astype(BF16), tm)
        h = _ffn(h, row(ffn_norm_g[layer]), w_ffn_up[layer].astype(BF16),
                 ffn_conv_w[layer].astype(F32), w_ffn_down[layer].astype(BF16), tm_ffn)
    return h[N_META:N_META + seq][None]
```

```python
import functools
import math

import jax
import jax.numpy as jnp
import ml_dtypes
import numpy as np
from jax import lax
from jax.experimental import pallas as pl
from jax.experimental.pallas import tpu as pltpu

D_MODEL = 1024
N_META = 16
A_HEADS = 8
A_DH = 64
A_DV = 2 * A_DH
DN_HEADS = 8
DN_DK = 128
DN_DV = 128
DN_CONV = 4
D_FF = 2816
FFN_CONV = 3
RMS_EPS = 1e-6
L2_EPS = 1e-6
ALIBI_MAX_EXP = 8.0

LANES = 128
SUBLANES = 8
ROW_ALIGN = 512
ATTN_TQ = 512
ATTN_TK = 256
ATTN_GROUP = 4
SAFE_TQ = 256
SAFE_TK_BIG = 1024
GDN_CHUNK = 128
BLK = 1024
LOG2E = 1.4426950408889634
VMEM_LIMIT = 52 * 1024 * 1024

FAST_EXP_BOUND = 60.0
ZERO_SCALE_EXP = 150.0

AUG_ROWB = 0
AUG_ROWA = 3
AUG_JR = 6
AUG_ONE = 9
AUG_JREL = 12
AUG_ONE2 = 15
N_PIECES = 3
assert ATTN_GROUP <= 4 and ATTN_TQ // ATTN_TK <= 2 and ATTN_TK == 256
AUG_BASE = (A_DH, 0)
N_QVAR = ATTN_GROUP + ATTN_TQ // ATTN_TK
QTAB_SLOPE_ROW = 2 * N_QVAR
QTAB_ROWS = 16

F32 = jnp.float32
BF16 = jnp.bfloat16
NT_DIMS = (((1,), (1,)), ((), ()))
TN_DIMS = (((0,), (0,)), ((), ()))


def _params(sem, vmem=VMEM_LIMIT):
    return pltpu.CompilerParams(dimension_semantics=sem, vmem_limit_bytes=vmem)


def _pick_tile(n, candidates):
    for c in candidates:
        if n % c == 0:
            return c
    raise ValueError(f"no tile in {candidates} divides {n}")


def _silu(x):
    return x * jax.nn.sigmoid(x)


def _bf16_pieces(x):
    out, r = [], np.float64(x)
    for _ in range(N_PIECES):
        p = float(np.float32(r).astype(ml_dtypes.bfloat16).astype(np.float32))
        out.append(p)
        r -= p
    return out


def _alibi_tables():
    qtab = np.zeros((A_HEADS, QTAB_ROWS, LANES), np.float32)
    ktab = np.zeros((2 * A_HEADS, LANES), np.float32)
    for h in range(A_HEADS):
        slope = 2.0 ** (-(ALIBI_MAX_EXP / A_HEADS) * (h + 1))
        pieces = _bf16_pieces(slope * LOG2E)
        qtab[h, QTAB_SLOPE_ROW, :] = np.float32(sum(pieces))
        for c in range(2):
            b = AUG_BASE[c]
            for i, p in enumerate(pieces):
                ktab[2 * h + c, b + AUG_ROWB + i] = -p
                ktab[2 * h + c, b + AUG_ROWA + i] = -float(ATTN_TK) * p
                ktab[2 * h + c, b + AUG_ONE + i] = 1.0
                ktab[2 * h + c, b + AUG_ONE2 + i] = 1.0
                for t in range(ATTN_GROUP):
                    after = ATTN_GROUP - 1 - t
                    qtab[h, N_QVAR * c + t, b + AUG_JR + i] = -p
                    qtab[h, N_QVAR * c + t, b + AUG_ONE + i] = -p * ATTN_TK * (after & 1)
                    qtab[h, N_QVAR * c + t, b + AUG_ONE2 + i] = -p * ATTN_TK * (after & 2)
                for d in range(ATTN_TQ // ATTN_TK):
                    qtab[h, N_QVAR * c + ATTN_GROUP + d, b + AUG_JREL + i] = p
                    qtab[h, N_QVAR * c + ATTN_GROUP + d, b + AUG_ONE + i] = p * ATTN_TK * d
    return jnp.asarray(qtab), jnp.asarray(ktab)


def _inproj_kernel(h_ref, g_ref, w_ref, ws_ref, o_ref, os_ref, u_ref):
    @pl.when(pl.program_id(1) == 0)
    def _():
        x = h_ref[...]
        ms = jnp.mean(x * x, axis=-1, keepdims=True)
        u = ((x * lax.rsqrt(ms + RMS_EPS)) * g_ref[...]).astype(BF16)
        u_ref[...] = u
        os_ref[...] = jnp.dot(u, ws_ref[...], preferred_element_type=F32)

    o_ref[...] = jnp.dot(u_ref[...], w_ref[...], preferred_element_type=F32).astype(o_ref.dtype)


def _inproj(h, g, w_main, w_small, tm, tn=BLK):
    lp = h.shape[0]
    n_main = w_main.shape[1]
    n_small = w_small.shape[1]
    return pl.pallas_call(
        _inproj_kernel,
        out_shape=(jax.ShapeDtypeStruct((lp, n_main), BF16),
                   jax.ShapeDtypeStruct((lp, n_small), F32)),
        grid=(lp // tm, n_main // tn),
        in_specs=[
            pl.BlockSpec((tm, D_MODEL), lambda m, n: (m, 0)),
            pl.BlockSpec((1, D_MODEL), lambda m, n: (0, 0)),
            pl.BlockSpec((D_MODEL, tn), lambda m, n: (0, n)),
            pl.BlockSpec((D_MODEL, n_small), lambda m, n: (0, 0)),
        ],
        out_specs=(
            pl.BlockSpec((tm, tn), lambda m, n: (m, n)),
            pl.BlockSpec((tm, n_small), lambda m, n: (m, 0)),
        ),
        scratch_shapes=[pltpu.VMEM((tm, D_MODEL), BF16)],
        compiler_params=_params(("parallel", "arbitrary")),
        name="inproj",
    )(h, g, w_main, w_small)


def _qkprep_kernel(aq_ref, ak_ref, av_ref, gq_ref, gk_ref, ktab_ref, q_out, k_out, v_out, stat_out):
    tm = aq_ref.shape[0]
    lane = lax.broadcasted_iota(jnp.int32, (tm, LANES), 1)
    lo = lane < A_DH
    q_scale = (A_DH ** -0.5) * LOG2E

    def norm_pair(x, g):
        ss = x * x
        s_lo = jnp.sum(jnp.where(lo, ss, 0.0), axis=-1, keepdims=True)
        s_hi = jnp.sum(jnp.where(lo, 0.0, ss), axis=-1, keepdims=True)
        ms = jnp.where(lo, s_lo, s_hi) * (1.0 / A_DH)
        return (x * lax.rsqrt(ms + RMS_EPS)) * g

    def max_sq_norms(y):
        yy = y * y
        n_lo = jnp.sum(jnp.where(lo, yy, 0.0), axis=-1, keepdims=True)
        n_hi = jnp.sum(jnp.where(lo, 0.0, yy), axis=-1, keepdims=True)
        return jnp.max(n_lo, axis=0, keepdims=True), jnp.max(n_hi, axis=0, keepdims=True)

    j = (pl.program_id(0) * tm + lax.broadcasted_iota(jnp.int32, (tm, LANES), 0))
    jrel = jnp.bitwise_and(j, ATTN_TK - 1)
    jr = (ATTN_TK - 1) - jrel
    pos = []
    for c in range(2):
        la = lane - AUG_BASE[c]
        pick = lambda off, la=la: (la >= off) & (la < off + N_PIECES)
        pos.append(jnp.where(pick(AUG_JR), jr, jnp.where(pick(AUG_JREL), jrel, 0)).astype(F32))

    lane1 = lax.broadcasted_iota(jnp.int32, (1, LANES), 1)
    stat_q = jnp.zeros((1, LANES), F32)
    stat_k = jnp.zeros((1, LANES), F32)
    for hh in range(A_HEADS):
        sl = slice(hh * LANES, (hh + 1) * LANES)
        yq = norm_pair(aq_ref[:, sl].astype(F32), gq_ref[...]) * q_scale
        yk = norm_pair(ak_ref[:, sl].astype(F32), gk_ref[...])
        q_out[:, sl] = yq.astype(BF16)
        k_out[:, 2 * hh * LANES:(2 * hh + 1) * LANES] = jnp.where(
            lo, yk, pos[0] + ktab_ref[2 * hh:2 * hh + 1, :]).astype(BF16)
        k_out[:, (2 * hh + 1) * LANES:(2 * hh + 2) * LANES] = jnp.where(
            lo, pos[1] + ktab_ref[2 * hh + 1:2 * hh + 2, :], yk).astype(BF16)
        for c, (mq, mk) in enumerate(zip(max_sq_norms(yq), max_sq_norms(yk))):
            stat_q = jnp.maximum(stat_q, jnp.where(lane1 == 2 * hh + c, mq, 0.0))
            stat_k = jnp.maximum(stat_k, jnp.where(lane1 == 2 * hh + c, mk, 0.0))
    v_out[...] = av_ref[...]

    @pl.when(pl.program_id(0) == 0)
    def _():
        stat_out[...] = jnp.zeros(stat_out.shape, F32)

    stat_out[0:1, :] = jnp.maximum(stat_out[0:1, :], stat_q)
    stat_out[1:2, :] = jnp.maximum(stat_out[1:2, :], stat_k)


def _qkprep(proj, gq2, gk2, ktab, tm):
    lp = proj.shape[0]
    blk = lambda j: pl.BlockSpec((tm, BLK), lambda m, j=j: (m, j))
    vec = pl.BlockSpec((1, LANES), lambda m: (0, 0))
    out = jax.ShapeDtypeStruct((lp, BLK), BF16)
    return pl.pallas_call(
        _qkprep_kernel,
        out_shape=(out, jax.ShapeDtypeStruct((lp, 2 * BLK), BF16), out,
                   jax.ShapeDtypeStruct((SUBLANES, LANES), F32)),
        grid=(lp // tm,),
        in_specs=[blk(0), blk(1), blk(2), vec, vec,
                  pl.BlockSpec((2 * A_HEADS, LANES), lambda m: (0, 0))],
        out_specs=(blk(0), pl.BlockSpec((tm, 2 * BLK), lambda m: (m, 0)), blk(0),
                   pl.BlockSpec((SUBLANES, LANES), lambda m: (0, 0))),
        compiler_params=_params(("arbitrary",)),
        name="qkprep",
    )(proj, proj, proj, gq2, gk2, ktab)


def _attn_finish(lq1_ref, lk1_ref, lq2_ref, lk2_ref, g_ref, o_ref, o1, o2, lam_init):
    lam = (jnp.exp(jnp.sum(lq1_ref[...] * lk1_ref[...], axis=-1, keepdims=True))
           - jnp.exp(jnp.sum(lq2_ref[...] * lk2_ref[...], axis=-1, keepdims=True))
           + lam_init)
    o = o1 - lam * o2
    ms = jnp.mean(o * o, axis=-1, keepdims=True)
    y = ((o * lax.rsqrt(ms + RMS_EPS)) * g_ref[...]) * (1.0 - lam_init)
    o_ref[...] = y.astype(BF16)


def _attn_fast_kernel(lq1_ref, lk1_ref, lq2_ref, lk2_ref, qtab_ref, q_ref, k_ref, v_ref, g_ref,
                      o_ref, qa_ref, l_ref, acc_ref, *, lam_init):
    tq, tk, grp = ATTN_TQ, ATTN_TK, ATTN_GROUP
    n_diag = tq // tk
    log_tk = int(math.log2(tk))
    h = pl.program_id(0)
    i = pl.program_id(1)
    row0 = i * tq
    tab = qtab_ref[0]
    slope2 = tab[QTAB_SLOPE_ROW:QTAB_SLOPE_ROW + 1, :]
    q = q_ref[...].astype(F32)
    lane = lax.broadcasted_iota(jnp.int32, (tq, LANES), 1)
    rowi = lax.broadcasted_iota(jnp.int32, (tq, LANES), 0)
    row_b = jnp.bitwise_and(rowi, tk - 1).astype(F32)
    row_a = jnp.right_shift(rowi, log_tk).astype(F32)
    tab16 = tab.astype(BF16)
    for c in range(2):
        content = (lane < A_DH) if c == 0 else (lane >= A_DH)
        la = lane - AUG_BASE[c]
        pick = lambda off, la=la: (la >= off) & (la < off + N_PIECES)
        rows = jnp.where(pick(AUG_ROWB), row_b, jnp.where(pick(AUG_ROWA), row_a, 0.0))
        base = jnp.where(content, q, rows).astype(BF16)
        for variant in range(N_QVAR):
            qa_ref[N_QVAR * c + variant] = base + tab16[N_QVAR * c + variant:N_QVAR * c + variant + 1, :]
    l_ref[...] = jnp.zeros(l_ref.shape, F32)
    acc_ref[...] = jnp.zeros(acc_ref.shape, F32)

    def scores(c, variant, start, rows=slice(None)):
        s = lax.dot_general(qa_ref[N_QVAR * c + variant, rows, :],
                            k_ref[pl.ds(start, tk), c * LANES:(c + 1) * LANES],
                            NT_DIMS, preferred_element_type=F32)
        return s

    def far_tiles(start, variants, scale_dist):
        scale = jnp.exp2(-slope2 * scale_dist.astype(F32))
        for c in range(2):
            pv = None
            lsum = None
            for t, variant in enumerate(variants):
                p = jnp.exp2(scores(c, variant, start + t * tk))
                part = p[:, :LANES] + p[:, LANES:]
                lsum = part if lsum is None else lsum + part
                d = jnp.dot(p.astype(BF16), v_ref[pl.ds(start + t * tk, tk), :],
                            preferred_element_type=F32)
                pv = d if pv is None else pv + d
            acc_ref[c] += pv * scale
            l_ref[c] += lsum * scale

    assert ALIBI_MAX_EXP == A_HEADS
    zero_dist = jnp.left_shift(jnp.int32(math.ceil(ZERO_SCALE_EXP / LOG2E * 2.0)), h)
    avail = jnp.right_shift(row0, log_tk)
    first_tile = jnp.right_shift(jnp.maximum(row0 - zero_dist, 0), log_tk)
    need = avail - first_tile
    n_groups = jnp.minimum((need + grp - 1) // grp, avail // grp)
    group_tile0 = avail - n_groups * grp
    n_single = jnp.maximum(group_tile0 - first_tile, 0)

    def single_body(js, carry):
        start = pl.multiple_of((first_tile + js) * tk, tk)
        far_tiles(start, (grp - 1,), row0 - (start + tk - 1))
        return carry

    lax.fori_loop(0, n_single, single_body, 0)

    def group_body(jg, carry):
        start = pl.multiple_of((group_tile0 + jg * grp) * tk, tk)
        far_tiles(start, tuple(range(grp)), row0 - (start + grp * tk - 1))
        return carry

    lax.fori_loop(0, n_groups, group_body, 0)

    for d in range(n_diag):
        rows = slice(d * tk, tq)
        n_rows = tq - d * tk
        keep = (lax.broadcasted_iota(jnp.int32, (1, tk), 1)
                <= lax.broadcasted_iota(jnp.int32, (n_rows, 1), 0))
        start = pl.multiple_of(row0 + d * tk, tk)
        for c in range(2):
            s = jnp.where(keep, scores(c, grp + d, start, rows), -jnp.inf)
            p = jnp.exp2(s)
            acc_ref[c, rows, :] += jnp.dot(p.astype(BF16), v_ref[pl.ds(start, tk), :],
                                           preferred_element_type=F32)
            l_ref[c, rows, :] += p[:, :LANES] + p[:, LANES:]

    o1 = acc_ref[0] / jnp.sum(l_ref[0], axis=-1, keepdims=True)
    o2 = acc_ref[1] / jnp.sum(l_ref[1], axis=-1, keepdims=True)
    _attn_finish(lq1_ref, lk1_ref, lq2_ref, lk2_ref, g_ref, o_ref, o1, o2, lam_init)


def _attn_safe_kernel(lq1_ref, lk1_ref, lq2_ref, lk2_ref, q_ref, k_ref, v_ref, g_ref, o_ref,
                      m_ref, l_ref, acc_ref, *, lam_init):
    tq, tb = SAFE_TQ, SAFE_TK_BIG
    h = pl.program_id(0)
    i = pl.program_id(1)
    slope2 = jnp.exp2(-(ALIBI_MAX_EXP / A_HEADS) * (h + 1).astype(F32)) * LOG2E
    q = q_ref[...].astype(F32)
    lane = lax.broadcasted_iota(jnp.int32, (tq, LANES), 1)
    q_maps = (jnp.where(lane < A_DH, q, 0.0).astype(BF16), jnp.where(lane >= A_DH, q, 0.0).astype(BF16))
    m_ref[...] = jnp.full(m_ref.shape, -jnp.inf, F32)
    l_ref[...] = jnp.zeros(l_ref.shape, F32)
    acc_ref[...] = jnp.zeros(acc_ref.shape, F32)
    row0 = i * tq

    def block(start, size, masked):
        v = v_ref[pl.ds(start, size), :]
        col = lax.broadcasted_iota(jnp.int32, (1, size), 1) + (start - row0)
        bias = slope2 * col.astype(F32)
        if masked:
            keep = col <= lax.broadcasted_iota(jnp.int32, (tq, 1), 0)
        for c in range(2):
            k = k_ref[pl.ds(start, size), c * LANES:(c + 1) * LANES]
            s = lax.dot_general(q_maps[c], k, NT_DIMS, preferred_element_type=F32) + bias
            if masked:
                s = jnp.where(keep, s, -jnp.inf)
            m_prev = m_ref[c]
            m_new = jnp.maximum(m_prev, jnp.max(s, axis=-1, keepdims=True))
            alpha = jnp.exp2(m_prev - m_new)
            p = jnp.exp2(s - m_new)
            l_ref[c] = alpha * l_ref[c] + jnp.sum(p, axis=-1, keepdims=True)
            acc_ref[c] = alpha * acc_ref[c] + jnp.dot(p.astype(BF16), v,
                                                      preferred_element_type=F32)
            m_ref[c] = m_new

    n_big = row0 // tb

    def big_body(jb, carry):
        block(pl.multiple_of(jb * tb, tb), tb, False)
        return carry

    lax.fori_loop(0, n_big, big_body, 0)
    base = n_big * tb
    n_small = (row0 - base) // tq

    def small_body(js, carry):
        block(pl.multiple_of(base + js * tq, tq), tq, False)
        return carry

    lax.fori_loop(0, n_small, small_body, 0)
    block(pl.multiple_of(row0, tq), tq, True)
    _attn_finish(lq1_ref, lk1_ref, lq2_ref, lk2_ref, g_ref, o_ref,
                 acc_ref[0] / l_ref[0], acc_ref[1] / l_ref[1], lam_init)


def _attn(qn, ka, vb, stat, qtab, lq1, lk1, lq2, lk2, g_sub, lam_init):
    lp = qn.shape[0]
    lamspec = pl.BlockSpec((1, A_DH), lambda h, i: (0, 0))
    lams = [lamspec] * 4

    def common(tq):
        return dict(
            out_shape=jax.ShapeDtypeStruct((lp, A_HEADS * A_DV), BF16),
            grid=(A_HEADS, lp // tq),
            out_specs=pl.BlockSpec((tq, A_DV), lambda h, i: (i, h)),
            compiler_params=_params(("parallel", "arbitrary")),
        )

    def data_specs(tq):
        return [pl.BlockSpec((tq, LANES), lambda h, i: (i, h)),
                pl.BlockSpec((lp, 2 * LANES), lambda h, i: (0, h)),
                pl.BlockSpec((lp, LANES), lambda h, i: (0, h)),
                pl.BlockSpec((1, A_DV), lambda h, i: (0, 0))]

    def fast(qn, ka, vb):
        tq = ATTN_TQ
        return pl.pallas_call(
            functools.partial(_attn_fast_kernel, lam_init=lam_init),
            in_specs=(lams + [pl.BlockSpec((1, QTAB_ROWS, LANES), lambda h, i: (h, 0, 0))]
                      + data_specs(tq)),
            scratch_shapes=[pltpu.VMEM((2 * N_QVAR, tq, LANES), BF16),
                            pltpu.VMEM((2, tq, LANES), F32), pltpu.VMEM((2, tq, A_DV), F32)],
            name="attn_fast", **common(tq),
        )(lq1, lk1, lq2, lk2, qtab, qn, ka, vb, g_sub)

    def safe(qn, ka, vb):
        tq = SAFE_TQ
        return pl.pallas_call(
            functools.partial(_attn_safe_kernel, lam_init=lam_init),
            in_specs=lams + data_specs(tq),
            scratch_shapes=[pltpu.VMEM((2, tq, 1), F32), pltpu.VMEM((2, tq, 1), F32),
                            pltpu.VMEM((2, tq, A_DV), F32)],
            name="attn_safe", **common(tq),
        )(lq1, lk1, lq2, lk2, qn, ka, vb, g_sub)

    bound = jnp.sqrt(jnp.max(stat[0] * stat[1]))
    return lax.cond(bound <= FAST_EXP_BOUND, fast, safe, qn, ka, vb)


def _gdnprep_kernel(xq_ref, xk_ref, xv_ref, cw_ref, sm_ref, alog_ref, dtb_ref,
                    q_out, k_out, v_out, beta_out, gcum_out, gcumt_out, xbuf_ref):
    tm = xq_ref.shape[0]
    halo = SUBLANES

    @pl.when(pl.program_id(0) == 0)
    def _():
        xbuf_ref[0:halo, :] = jnp.zeros((halo, 3 * BLK), F32)

    xbuf_ref[halo:, 0:BLK] = xq_ref[...].astype(F32)
    xbuf_ref[halo:, BLK:2 * BLK] = xk_ref[...].astype(F32)
    xbuf_ref[halo:, 2 * BLK:] = xv_ref[...].astype(F32)

    outs = (q_out, k_out, v_out)
    for s in range(3 * BLK // LANES):
        sl = slice(s * LANES, (s + 1) * LANES)
        xw = xbuf_ref[:, sl]
        acc = xw[halo:] * cw_ref[DN_CONV - 1:DN_CONV, sl]
        for back in range(1, DN_CONV):
            shifted = pltpu.roll(xw, back, axis=0)[halo:]
            acc = acc + shifted * cw_ref[DN_CONV - 1 - back:DN_CONV - back, sl]
        y = _silu(acc)
        part, hh = divmod(s, DN_HEADS)
        if part < 2:
            y = y * lax.rsqrt(jnp.sum(y * y, axis=-1, keepdims=True) + L2_EPS)
            if part == 0:
                y = y * (DN_DK ** -0.5)
        outs[part][:, hh * LANES:(hh + 1) * LANES] = y.astype(BF16)

    xbuf_ref[0:halo, :] = xbuf_ref[tm:tm + halo, :]

    beta_out[...] = jax.nn.sigmoid(sm_ref[:, 0:LANES])
    g = -jnp.exp(alog_ref[...]) * jax.nn.softplus(sm_ref[:, LANES:] + dtb_ref[...])
    g1 = g.astype(BF16)
    r1 = g - g1.astype(F32)
    g2 = r1.astype(BF16)
    g3 = (r1 - g2.astype(F32)).astype(BF16)
    ii = lax.broadcasted_iota(jnp.int32, (GDN_CHUNK, GDN_CHUNK), 0)
    jj = lax.broadcasted_iota(jnp.int32, (GDN_CHUNK, GDN_CHUNK), 1)
    tril = (ii >= jj).astype(BF16)
    for cidx in range(tm // GDN_CHUNK):
        rs = slice(cidx * GDN_CHUNK, (cidx + 1) * GDN_CHUNK)
        gc = (jnp.dot(tril, g1[rs], preferred_element_type=F32)
              + jnp.dot(tril, g2[rs], preferred_element_type=F32)
              + jnp.dot(tril, g3[rs], preferred_element_type=F32))
        gcum_out[rs, :] = gc
        gcumt_out[:, rs] = gc.T[0:SUBLANES, :]


def _gdnprep(proj, small, conv_w, alog, dtb, tm):
    lp = proj.shape[0]
    blk = lambda j: pl.BlockSpec((tm, BLK), lambda m, j=j: (m, j))
    row = pl.BlockSpec((tm, LANES), lambda m: (m, 0))
    vec = pl.BlockSpec((1, LANES), lambda m: (0, 0))
    o16 = jax.ShapeDtypeStruct((lp, BLK), BF16)
    o32 = jax.ShapeDtypeStruct((lp, LANES), F32)
    return pl.pallas_call(
        _gdnprep_kernel,
        out_shape=(o16, o16, o16, o32, o32, jax.ShapeDtypeStruct((SUBLANES, lp), F32)),
        grid=(lp // tm,),
        in_specs=[blk(3), blk(4), blk(5),
                  pl.BlockSpec((DN_CONV, 3 * BLK), lambda m: (0, 0)),
                  pl.BlockSpec((tm, 2 * LANES), lambda m: (m, 0)), vec, vec],
        out_specs=(blk(0), blk(0), blk(0), row, row,
                   pl.BlockSpec((SUBLANES, tm), lambda m: (0, m))),
        scratch_shapes=[pltpu.VMEM((tm + SUBLANES, 3 * BLK), F32)],
        compiler_params=_params(("arbitrary",)),
        name="gdnprep",
    )(proj, proj, proj, conv_w, small, alog, dtb)


def _gdn_kernel(q_ref, k_ref, v_ref, beta_ref, gc_ref, gt_ref, z_ref, gn_ref, o_ref, s_ref):
    c = GDN_CHUNK
    heads = range(DN_HEADS)

    @pl.when(pl.program_id(0) == 0)
    def _():
        s_ref[...] = jnp.zeros(s_ref.shape, F32)

    ii = lax.broadcasted_iota(jnp.int32, (c, c), 0)
    jj = lax.broadcasted_iota(jnp.int32, (c, c), 1)
    ge = ii >= jj
    gt = ii > jj
    eye = (ii == jj).astype(F32)
    blk_id = lambda t: (jnp.right_shift(ii, t), jnp.right_shift(jj, t))
    same = lambda t: blk_id(t)[0] == blk_id(t)[1]
    pair_mask = same(1)
    merge_masks = [same(t + 1) & jnp.logical_not(same(t)) for t in range(1, int(math.log2(c)))]

    def mm(a, b):
        return jnp.dot(a.astype(BF16), b.astype(BF16), preferred_element_type=F32)

    sl = [slice(hh * LANES, (hh + 1) * LANES) for hh in heads]
    q16 = [q_ref[:, sl[hh]] for hh in heads]
    k16 = [k_ref[:, sl[hh]] for hh in heads]
    k = [x.astype(F32) for x in k16]
    beta = [beta_ref[:, hh:hh + 1] for hh in heads]
    g_col = [gc_ref[:, hh:hh + 1] for hh in heads]
    g_last = [gc_ref[c - 1:c, hh:hh + 1] for hh in heads]
    decay = [jnp.exp(jnp.where(ge, g_col[hh] - gt_ref[hh:hh + 1, :], -jnp.inf)) for hh in heads]
    e_g = [jnp.exp(g) for g in g_col]
    kb = [k[hh] * beta[hh] for hh in heads]
    vb = [v_ref[:, sl[hh]].astype(F32) * beta[hh] for hh in heads]
    kq = [lax.dot_general(jnp.concatenate([kb[hh].astype(BF16), q16[hh]], axis=0), k16[hh],
                          NT_DIMS, preferred_element_type=F32) for hh in heads]
    a = [jnp.where(gt, kq[hh][:c] * decay[hh], 0.0) for hh in heads]
    qk = [jnp.where(ge, kq[hh][c:] * decay[hh], 0.0) for hh in heads]
    x = [eye - jnp.where(pair_mask, a[hh], 0.0) for hh in heads]
    for mask in merge_masks:
        y = [mm(jnp.where(mask, a[hh], 0.0), x[hh]) for hh in heads]
        x = [x[hh] - mm(x[hh], y[hh]) for hh in heads]
    uw = [mm(x[hh], jnp.concatenate([vb[hh], kb[hh] * e_g[hh]], axis=1)) for hh in heads]
    s_old = [s_ref[hh] for hh in heads]
    ws = [mm(jnp.concatenate([uw[hh][:, DN_DV:], q16[hh].astype(F32) * e_g[hh]], axis=0), s_old[hh])
          for hh in heads]
    v_new = [uw[hh][:, :DN_DV] - ws[hh][:c] for hh in heads]
    o = [ws[hh][c:] + mm(qk[hh], v_new[hh]) for hh in heads]
    for hh in heads:
        kd = k[hh] * jnp.exp(g_last[hh] - g_col[hh])
        s_ref[hh] = s_old[hh] * jnp.exp(g_last[hh]) + lax.dot_general(
            kd.astype(BF16), v_new[hh].astype(BF16), TN_DIMS, preferred_element_type=F32)
    for hh in heads:
        ms = jnp.mean(o[hh] * o[hh], axis=-1, keepdims=True)
        y = ((o[hh] * lax.rsqrt(ms + RMS_EPS)) * gn_ref[...]) * _silu(z_ref[:, sl[hh]].astype(F32))
        o_ref[:, sl[hh]] = y.astype(BF16)


def _gdn(qn, kn, vn, beta, gcum, gcum_t, proj, gn):
    lp = qn.shape[0]
    c = GDN_CHUNK
    blk = lambda j: pl.BlockSpec((c, BLK), lambda n, j=j: (n, j))
    row = pl.BlockSpec((c, LANES), lambda n: (n, 0))
    return pl.pallas_call(
        _gdn_kernel,
        out_shape=jax.ShapeDtypeStruct((lp, DN_HEADS * DN_DV), BF16),
        grid=(lp // c,),
        in_specs=[blk(0), blk(0), blk(0), row, row,
                  pl.BlockSpec((SUBLANES, c), lambda n: (0, n)),
                  blk(6), pl.BlockSpec((1, DN_DV), lambda n: (0, 0))],
        out_specs=blk(0),
        scratch_shapes=[pltpu.VMEM((DN_HEADS, DN_DK, DN_DV), F32)],
        compiler_params=_params(("arbitrary",)),
        name="gdn",
    )(qn, kn, vn, beta, gcum, gcum_t, proj, gn)


def _mix_kernel(ao_ref, do_ref, ga_ref, gb_ref, h_ref, wa_ref, wb_ref, wo_ref, o_ref):
    ya = jnp.dot(ao_ref[...], wa_ref[...], preferred_element_type=F32)
    yb = jnp.dot(do_ref[...], wb_ref[...], preferred_element_type=F32)
    mixed = (jax.nn.sigmoid(ga_ref[...].astype(F32)) * ya
             + jax.nn.sigmoid(gb_ref[...].astype(F32)) * yb)
    o_ref[...] = h_ref[...] + jnp.dot(mixed.astype(BF16), wo_ref[...],
                                      preferred_element_type=F32)


def _mix(ao, do, proj, h, wa, wb, wo, tm):
    lp = h.shape[0]
    blk = lambda j: pl.BlockSpec((tm, BLK), lambda m, j=j: (m, j))
    wspec = pl.BlockSpec((BLK, D_MODEL), lambda m: (0, 0))
    return pl.pallas_call(
        _mix_kernel,
        out_shape=jax.ShapeDtypeStruct((lp, D_MODEL), F32),
        grid=(lp // tm,),
        in_specs=[blk(0), blk(0), blk(7), blk(8), blk(0), wspec, wspec, wspec],
        out_specs=blk(0),
        compiler_params=_params(("parallel",)),
        name="mix",
    )(ao, do, proj, proj, h, wa, wb, wo)


def _ffn_kernel(h_ref, g_ref, wup_ref, cw_ref, wdn_ref, o_ref, fbuf_ref, *, tc):
    tm = h_ref.shape[0]
    halo = SUBLANES

    @pl.when(pl.program_id(0) == 0)
    def _():
        fbuf_ref[0:halo, :] = jnp.zeros((halo, 2 * D_FF), F32)

    @pl.when(pl.program_id(0) > 0)
    def _():
        fbuf_ref[0:halo, :] = fbuf_ref[tm:tm + halo, :]

    x = h_ref[...]
    ms = jnp.mean(x * x, axis=-1, keepdims=True)
    u = ((x * lax.rsqrt(ms + RMS_EPS)) * g_ref[...]).astype(BF16)
    for j in range(2 * D_FF // tc):
        sl = slice(j * tc, (j + 1) * tc)
        fbuf_ref[halo:, sl] = jnp.dot(u, wup_ref[:, sl], preferred_element_type=F32)

    def conv(sl):
        acc = None
        for t in range(FFN_CONV):
            off = halo - (FFN_CONV - 1) + t
            term = fbuf_ref[off:off + tm, sl] * cw_ref[t:t + 1, sl]
            acc = term if acc is None else acc + term
        return acc

    acc = x
    for j in range(D_FF // tc):
        gate = conv(slice(j * tc, (j + 1) * tc))
        up = conv(slice(D_FF + j * tc, D_FF + (j + 1) * tc))
        act = (_silu(gate) * up).astype(BF16)
        acc = acc + jnp.dot(act, wdn_ref[j * tc:(j + 1) * tc, :], preferred_element_type=F32)
    o_ref[...] = acc


def _ffn(h, g, wup, cw, wdn, tm, tc=256):
    lp = h.shape[0]
    once = pl.Buffered(1)
    return pl.pallas_call(
        functools.partial(_ffn_kernel, tc=tc),
        out_shape=jax.ShapeDtypeStruct((lp, D_MODEL), F32),
        grid=(lp // tm,),
        in_specs=[pl.BlockSpec((tm, D_MODEL), lambda m: (m, 0)),
                  pl.BlockSpec((1, D_MODEL), lambda m: (0, 0)),
                  pl.BlockSpec((D_MODEL, 2 * D_FF), lambda m: (0, 0), pipeline_mode=once),
                  pl.BlockSpec((FFN_CONV, 2 * D_FF), lambda m: (0, 0)),
                  pl.BlockSpec((D_FF, D_MODEL), lambda m: (0, 0), pipeline_mode=once)],
        out_specs=pl.BlockSpec((tm, D_MODEL), lambda m: (m, 0)),
        scratch_shapes=[pltpu.VMEM((tm + SUBLANES, 2 * D_FF), F32)],
        compiler_params=_params(("arbitrary",)),
        name="ffn",
    )(h, g, wup, cw, wdn)


def _regroup_w_in(w):
    a_cols = 3 * A_HEADS * A_DV
    dn_cols = DN_HEADS * (2 * DN_DK + DN_DV) + DN_HEADS * DN_DV
    o_small = a_cols + dn_cols
    w_main = jnp.concatenate([w[:, :o_small], w[:, o_small + 2 * DN_HEADS:]], axis=1)
    w_small = jnp.zeros((D_MODEL, 2 * LANES), w.dtype)
    w_small = w_small.at[:, 0:DN_HEADS].set(w[:, o_small:o_small + DN_HEADS])
    w_small = w_small.at[:, LANES:LANES + DN_HEADS].set(
        w[:, o_small + DN_HEADS:o_small + 2 * DN_HEADS])
    return w_main.astype(BF16), w_small.astype(BF16)


def _pad_lanes(v):
    return jnp.zeros((1, LANES), F32).at[0, :v.shape[0]].set(v.astype(F32))


def kernel(x, meta_tokens, mix_norm_g, w_in, q_norm_g, k_norm_g, lambda_q1, lambda_k1, lambda_q2, lambda_k2, attn_subln_g, dn_conv_w, dn_a_log, dn_dt_bias, dn_norm_g, w_branch_attn, w_branch_dn, w_out, ffn_norm_g, w_ffn_up, ffn_conv_w, w_ffn_down):
    batch, seq, _ = x.shape
    assert batch == 1
    depth = w_in.shape[0]
    length = N_META + seq
    lp = -(-length // ROW_ALIGN) * ROW_ALIGN
    tm = _pick_tile(lp, (768, 512))
    tm_ffn = _pick_tile(lp, (384, 256))
    qtab, ktab = _alibi_tables()
    h = jnp.concatenate([meta_tokens.astype(F32), x[0],
                         jnp.zeros((lp - length, D_MODEL), F32)], axis=0)
    row = lambda v: v.astype(F32)[None, :]
    for layer in range(depth):
        lam_init = 0.8 - 0.6 * math.exp(-0.3 * layer)
        w_main, w_small = _regroup_w_in(w_in[layer])
        proj, small = _inproj(h, row(mix_norm_g[layer]), w_main, w_small, tm)

        gq2 = jnp.tile(row(q_norm_g[layer]), (1, 2))
        gk2 = jnp.tile(row(k_norm_g[layer]), (1, 2))
        qn, ka, vb, stat = _qkprep(proj, gq2, gk2, ktab, tm)
        ao = _attn(qn, ka, vb, stat, qtab, row(lambda_q1[layer]), row(lambda_k1[layer]),
                   row(lambda_q2[layer]), row(lambda_k2[layer]), row(attn_subln_g[layer]),
                   lam_init)

        dq, dk, dv, beta, gcum, gcum_t = _gdnprep(
            proj, small, dn_conv_w[layer].astype(F32), _pad_lanes(dn_a_log[layer]),
            _pad_lanes(dn_dt_bias[layer]), tm)
        do = _gdn(dq, dk, dv, beta, gcum, gcum_t, proj, row(dn_norm_g[layer]))

        h = _mix(ao, do, proj, h, w_branch_attn[layer].astype(BF16),
                 w_branch_dn[layer].astype(BF16), w_out[layer].astype(BF16), tm)
        h = _ffn(h, row(ffn_norm_g[layer]), w_ffn_up[layer].astype(BF16),
                 ffn_conv_w[layer].astype(F32), w_ffn_down[layer].astype(BF16), tm_ffn)
    return h[N_META:N_META + seq][None]
```

```python
import functools
import math

import jax
import jax.numpy as jnp
import ml_dtypes
import numpy as np
from jax import lax
from jax.experimental import pallas as pl
from jax.experimental.pallas import tpu as pltpu

D_MODEL = 1024
N_META = 16
A_HEADS = 8
A_DH = 64
A_DV = 2 * A_DH
DN_HEADS = 8
DN_DK = 128
DN_DV = 128
DN_CONV = 4
D_FF = 2816
FFN_CONV = 3
RMS_EPS = 1e-6
L2_EPS = 1e-6
ALIBI_MAX_EXP = 8.0

LANES = 128
SUBLANES = 8
ROW_ALIGN = 512
ATTN_TQ = 512
ATTN_TK = 256
ATTN_GROUP = 8
ATTN_HALF_GROUP = 4
SAFE_TQ = 256
SAFE_TK_BIG = 1024
GDN_CHUNK = 128
GDN_CHUNKS_PER_STEP = 2
BLK = 1024
LOG2E = 1.4426950408889634
VMEM_LIMIT = 52 * 1024 * 1024

FAST_EXP_BOUND = 60.0
ZERO_SCALE_EXP = 150.0

AUG_ROWB = 0
AUG_ROWA = 3
AUG_JR = 6
AUG_ONE = 9
AUG_JREL = 12
AUG_ONE2 = 15
AUG_ONE4 = 18
N_PIECES = 3
assert ATTN_GROUP <= 8 and ATTN_TQ // ATTN_TK <= 2 and ATTN_TK == 256
AUG_BASE = (A_DH, 0)
N_QVAR = ATTN_GROUP + ATTN_TQ // ATTN_TK
QTAB_SLOPE_ROW = 2 * N_QVAR
QTAB_ROWS = 24

F32 = jnp.float32
BF16 = jnp.bfloat16
NT_DIMS = (((1,), (1,)), ((), ()))
TN_DIMS = (((0,), (0,)), ((), ()))


def _params(sem, vmem=VMEM_LIMIT):
    return pltpu.CompilerParams(dimension_semantics=sem, vmem_limit_bytes=vmem)


def _pick_tile(n, candidates):
    for c in candidates:
        if n % c == 0:
            return c
    raise ValueError(f"no tile in {candidates} divides {n}")


def _silu(x):
    return x * jax.nn.sigmoid(x)


def _bf16_pieces(x):
    out, r = [], np.float64(x)
    for _ in range(N_PIECES):
        p = float(np.float32(r).astype(ml_dtypes.bfloat16).astype(np.float32))
        out.append(p)
        r -= p
    return out


def _alibi_tables():
    qtab = np.zeros((A_HEADS, QTAB_ROWS, LANES), np.float32)
    ktab = np.zeros((2 * A_HEADS, LANES), np.float32)
    for h in range(A_HEADS):
        slope = 2.0 ** (-(ALIBI_MAX_EXP / A_HEADS) * (h + 1))
        pieces = _bf16_pieces(slope * LOG2E)
        qtab[h, QTAB_SLOPE_ROW, :] = np.float32(sum(pieces))
        for c in range(2):
            b = AUG_BASE[c]
            for i, p in enumerate(pieces):
                ktab[2 * h + c, b + AUG_ROWB + i] = -p
                ktab[2 * h + c, b + AUG_ROWA + i] = -float(ATTN_TK) * p
                ktab[2 * h + c, b + AUG_ONE + i] = 1.0
                ktab[2 * h + c, b + AUG_ONE2 + i] = 1.0
                ktab[2 * h + c, b + AUG_ONE4 + i] = 1.0
                for t in range(ATTN_GROUP):
                    after = ATTN_GROUP - 1 - t
                    qtab[h, N_QVAR * c + t, b + AUG_JR + i] = -p
                    qtab[h, N_QVAR * c + t, b + AUG_ONE + i] = -p * ATTN_TK * (after & 1)
                    qtab[h, N_QVAR * c + t, b + AUG_ONE2 + i] = -p * ATTN_TK * (after & 2)
                    qtab[h, N_QVAR * c + t, b + AUG_ONE4 + i] = -p * ATTN_TK * (after & 4)
                for d in range(ATTN_TQ // ATTN_TK):
                    qtab[h, N_QVAR * c + ATTN_GROUP + d, b + AUG_JREL + i] = p
                    qtab[h, N_QVAR * c + ATTN_GROUP + d, b + AUG_ONE + i] = p * ATTN_TK * d
    return jnp.asarray(qtab), jnp.asarray(ktab)


def _inproj_kernel(h_ref, g_ref, w_ref, ws_ref, o_ref, os_ref, u_ref):
    @pl.when(pl.program_id(1) == 0)
    def _():
        x = h_ref[...]
        ms = jnp.mean(x * x, axis=-1, keepdims=True)
        u = ((x * lax.rsqrt(ms + RMS_EPS)) * g_ref[...]).astype(BF16)
        u_ref[...] = u
        os_ref[...] = jnp.dot(u, ws_ref[...], preferred_element_type=F32)

    o_ref[...] = jnp.dot(u_ref[...], w_ref[...], preferred_element_type=F32).astype(o_ref.dtype)


def _inproj(h, g, w_main, w_small, tm, tn=BLK):
    lp = h.shape[0]
    n_main = w_main.shape[1]
    n_small = w_small.shape[1]
    return pl.pallas_call(
        _inproj_kernel,
        out_shape=(jax.ShapeDtypeStruct((lp, n_main), BF16),
                   jax.ShapeDtypeStruct((lp, n_small), F32)),
        grid=(lp // tm, n_main // tn),
        in_specs=[
            pl.BlockSpec((tm, D_MODEL), lambda m, n: (m, 0)),
            pl.BlockSpec((1, D_MODEL), lambda m, n: (0, 0)),
            pl.BlockSpec((D_MODEL, tn), lambda m, n: (0, n)),
            pl.BlockSpec((D_MODEL, n_small), lambda m, n: (0, 0)),
        ],
        out_specs=(
            pl.BlockSpec((tm, tn), lambda m, n: (m, n)),
            pl.BlockSpec((tm, n_small), lambda m, n: (m, 0)),
        ),
        scratch_shapes=[pltpu.VMEM((tm, D_MODEL), BF16)],
        compiler_params=_params(("parallel", "arbitrary")),
        name="inproj",
    )(h, g, w_main, w_small)


def _qkprep_kernel(aq_ref, ak_ref, av_ref, gq_ref, gk_ref, ktab_ref, q_out, k_out, v_out):
    tm = aq_ref.shape[0]
    lane = lax.broadcasted_iota(jnp.int32, (tm, LANES), 1)
    lo = lane < A_DH
    q_scale = (A_DH ** -0.5) * LOG2E

    def norm_pair(x, g):
        ss = x * x
        s_lo = jnp.sum(jnp.where(lo, ss, 0.0), axis=-1, keepdims=True)
        s_hi = jnp.sum(jnp.where(lo, 0.0, ss), axis=-1, keepdims=True)
        ms = jnp.where(lo, s_lo, s_hi) * (1.0 / A_DH)
        return (x * lax.rsqrt(ms + RMS_EPS)) * g

    j = (pl.program_id(0) * tm + lax.broadcasted_iota(jnp.int32, (tm, LANES), 0))
    jrel = jnp.bitwise_and(j, ATTN_TK - 1)
    jr = (ATTN_TK - 1) - jrel
    pos = []
    for c in range(2):
        la = lane - AUG_BASE[c]
        pick = lambda off, la=la: (la >= off) & (la < off + N_PIECES)
        pos.append(jnp.where(pick(AUG_JR), jr, jnp.where(pick(AUG_JREL), jrel, 0)).astype(F32))

    for hh in range(A_HEADS):
        sl = slice(hh * LANES, (hh + 1) * LANES)
        yq = norm_pair(aq_ref[:, sl].astype(F32), gq_ref[...]) * q_scale
        yk = norm_pair(ak_ref[:, sl].astype(F32), gk_ref[...])
        q_out[:, sl] = yq.astype(BF16)
        k_out[:, 2 * hh * LANES:(2 * hh + 1) * LANES] = jnp.where(
            lo, yk, pos[0] + ktab_ref[2 * hh:2 * hh + 1, :]).astype(BF16)
        k_out[:, (2 * hh + 1) * LANES:(2 * hh + 2) * LANES] = jnp.where(
            lo, pos[1] + ktab_ref[2 * hh + 1:2 * hh + 2, :], yk).astype(BF16)
    v_out[...] = av_ref[...]


def _qkprep(proj, gq2, gk2, ktab, tm):
    lp = proj.shape[0]
    blk = lambda j: pl.BlockSpec((tm, BLK), lambda m, j=j: (m, j))
    vec = pl.BlockSpec((1, LANES), lambda m: (0, 0))
    out = jax.ShapeDtypeStruct((lp, BLK), BF16)
    return pl.pallas_call(
        _qkprep_kernel,
        out_shape=(out, jax.ShapeDtypeStruct((lp, 2 * BLK), BF16), out),
        grid=(lp // tm,),
        in_specs=[blk(0), blk(1), blk(2), vec, vec,
                  pl.BlockSpec((2 * A_HEADS, LANES), lambda m: (0, 0))],
        out_specs=(blk(0), pl.BlockSpec((tm, 2 * BLK), lambda m: (m, 0)), blk(0)),
        compiler_params=_params(("parallel",)),
        name="qkprep",
    )(proj, proj, proj, gq2, gk2, ktab)


def _attn_finish(lq1_ref, lk1_ref, lq2_ref, lk2_ref, g_ref, o_ref, o1, o2, lam_init):
    lam = (jnp.exp(jnp.sum(lq1_ref[...] * lk1_ref[...], axis=-1, keepdims=True))
           - jnp.exp(jnp.sum(lq2_ref[...] * lk2_ref[...], axis=-1, keepdims=True))
           + lam_init)
    o = o1 - lam * o2
    ms = jnp.mean(o * o, axis=-1, keepdims=True)
    y = ((o * lax.rsqrt(ms + RMS_EPS)) * g_ref[...]) * (1.0 - lam_init)
    o_ref[...] = y.astype(BF16)


def _attn_fast_kernel(lq1_ref, lk1_ref, lq2_ref, lk2_ref, qtab_ref, q_ref, k_ref, v_ref, g_ref,
                      o_ref, qa_ref, l_ref, acc_ref, *, lam_init):
    tq, tk, grp = ATTN_TQ, ATTN_TK, ATTN_GROUP
    n_diag = tq // tk
    log_tk = int(math.log2(tk))
    h = pl.program_id(0)
    i = pl.program_id(1)
    row0 = i * tq
    tab = qtab_ref[0]
    slope2 = tab[QTAB_SLOPE_ROW:QTAB_SLOPE_ROW + 1, :]
    q = q_ref[...].astype(F32)
    lane = lax.broadcasted_iota(jnp.int32, (tq, LANES), 1)
    rowi = lax.broadcasted_iota(jnp.int32, (tq, LANES), 0)
    row_b = jnp.bitwise_and(rowi, tk - 1).astype(F32)
    row_a = jnp.right_shift(rowi, log_tk).astype(F32)
    tab16 = tab.astype(BF16)
    for c in range(2):
        content = (lane < A_DH) if c == 0 else (lane >= A_DH)
        la = lane - AUG_BASE[c]
        pick = lambda off, la=la: (la >= off) & (la < off + N_PIECES)
        rows = jnp.where(pick(AUG_ROWB), row_b, jnp.where(pick(AUG_ROWA), row_a, 0.0))
        base = jnp.where(content, q, rows).astype(BF16)
        for variant in range(N_QVAR):
            qa_ref[N_QVAR * c + variant] = base + tab16[N_QVAR * c + variant:N_QVAR * c + variant + 1, :]
    l_ref[...] = jnp.zeros(l_ref.shape, F32)
    acc_ref[...] = jnp.zeros(acc_ref.shape, F32)

    def scores(c, variant, start, rows=slice(None)):
        s = lax.dot_general(qa_ref[N_QVAR * c + variant, rows, :],
                            k_ref[pl.ds(start, tk), c * LANES:(c + 1) * LANES],
                            NT_DIMS, preferred_element_type=F32)
        return s

    def far_tiles(start, variants, scale_dist):
        scale = jnp.exp2(-slope2 * scale_dist.astype(F32))
        for c in range(2):
            pv = None
            lsum = None
            for t, variant in enumerate(variants):
                p = jnp.exp2(scores(c, variant, start + t * tk))
                part = p[:, :LANES] + p[:, LANES:]
                lsum = part if lsum is None else lsum + part
                d = jnp.dot(p.astype(BF16), v_ref[pl.ds(start + t * tk, tk), :],
                            preferred_element_type=F32)
                pv = d if pv is None else pv + d
            acc_ref[c] += pv * scale
            l_ref[c] += lsum * scale

    assert ALIBI_MAX_EXP == A_HEADS
    zero_dist = jnp.left_shift(jnp.int32(math.ceil(ZERO_SCALE_EXP / LOG2E * 2.0)), h)
    avail = jnp.right_shift(row0, log_tk)
    first_tile = jnp.right_shift(jnp.maximum(row0 - zero_dist, 0), log_tk)
    need = avail - first_tile
    half = ATTN_HALF_GROUP
    n_groups = jnp.minimum((need + grp - half - 1) // grp, avail // grp)
    group_tile0 = avail - n_groups * grp
    left = jnp.maximum(group_tile0 - first_tile, 0)
    n_half = jnp.where((left > 0) & (group_tile0 >= half), 1, 0)
    half_tile0 = group_tile0 - n_half * half
    n_single = jnp.maximum(half_tile0 - first_tile, 0)

    def single_body(js, carry):
        start = pl.multiple_of((first_tile + js) * tk, tk)
        far_tiles(start, (grp - 1,), row0 - (start + tk - 1))
        return carry

    lax.fori_loop(0, n_single, single_body, 0)

    def half_body(jh, carry):
        start = pl.multiple_of(half_tile0 * tk, tk)
        far_tiles(start, tuple(range(grp - half, grp)), row0 - (start + half * tk - 1))
        return carry

    lax.fori_loop(0, n_half, half_body, 0)

    def group_body(jg, carry):
        start = pl.multiple_of((group_tile0 + jg * grp) * tk, tk)
        far_tiles(start, tuple(range(grp)), row0 - (start + grp * tk - 1))
        return carry

    lax.fori_loop(0, n_groups, group_body, 0)

    for d in range(n_diag):
        rows = slice(d * tk, tq)
        n_rows = tq - d * tk
        keep = (lax.broadcasted_iota(jnp.int32, (1, tk), 1)
                <= lax.broadcasted_iota(jnp.int32, (n_rows, 1), 0))
        start = pl.multiple_of(row0 + d * tk, tk)
        for c in range(2):
            s = jnp.where(keep, scores(c, grp + d, start, rows), -jnp.inf)
            p = jnp.exp2(s)
            acc_ref[c, rows, :] += jnp.dot(p.astype(BF16), v_ref[pl.ds(start, tk), :],
                                           preferred_element_type=F32)
            l_ref[c, rows, :] += p[:, :LANES] + p[:, LANES:]

    o1 = acc_ref[0] / jnp.sum(l_ref[0], axis=-1, keepdims=True)
    o2 = acc_ref[1] / jnp.sum(l_ref[1], axis=-1, keepdims=True)
    _attn_finish(lq1_ref, lk1_ref, lq2_ref, lk2_ref, g_ref, o_ref, o1, o2, lam_init)


def _attn_safe_kernel(lq1_ref, lk1_ref, lq2_ref, lk2_ref, q_ref, k_ref, v_ref, g_ref, o_ref,
                      m_ref, l_ref, acc_ref, *, lam_init):
    tq, tb = SAFE_TQ, SAFE_TK_BIG
    h = pl.program_id(0)
    i = pl.program_id(1)
    slope2 = jnp.exp2(-(ALIBI_MAX_EXP / A_HEADS) * (h + 1).astype(F32)) * LOG2E
    q = q_ref[...].astype(F32)
    lane = lax.broadcasted_iota(jnp.int32, (tq, LANES), 1)
    q_maps = (jnp.where(lane < A_DH, q, 0.0).astype(BF16), jnp.where(lane >= A_DH, q, 0.0).astype(BF16))
    m_ref[...] = jnp.full(m_ref.shape, -jnp.inf, F32)
    l_ref[...] = jnp.zeros(l_ref.shape, F32)
    acc_ref[...] = jnp.zeros(acc_ref.shape, F32)
    row0 = i * tq

    def block(start, size, masked):
        v = v_ref[pl.ds(start, size), :]
        col = lax.broadcasted_iota(jnp.int32, (1, size), 1) + (start - row0)
        bias = slope2 * col.astype(F32)
        if masked:
            keep = col <= lax.broadcasted_iota(jnp.int32, (tq, 1), 0)
        for c in range(2):
            k = k_ref[pl.ds(start, size), c * LANES:(c + 1) * LANES]
            s = lax.dot_general(q_maps[c], k, NT_DIMS, preferred_element_type=F32) + bias
            if masked:
                s = jnp.where(keep, s, -jnp.inf)
            m_prev = m_ref[c]
            m_new = jnp.maximum(m_prev, jnp.max(s, axis=-1, keepdims=True))
            alpha = jnp.exp2(m_prev - m_new)
            p = jnp.exp2(s - m_new)
            l_ref[c] = alpha * l_ref[c] + jnp.sum(p, axis=-1, keepdims=True)
            acc_ref[c] = alpha * acc_ref[c] + jnp.dot(p.astype(BF16), v,
                                                      preferred_element_type=F32)
            m_ref[c] = m_new

    n_big = row0 // tb

    def big_body(jb, carry):
        block(pl.multiple_of(jb * tb, tb), tb, False)
        return carry

    lax.fori_loop(0, n_big, big_body, 0)
    base = n_big * tb
    n_small = (row0 - base) // tq

    def small_body(js, carry):
        block(pl.multiple_of(base + js * tq, tq), tq, False)
        return carry

    lax.fori_loop(0, n_small, small_body, 0)
    block(pl.multiple_of(row0, tq), tq, True)
    _attn_finish(lq1_ref, lk1_ref, lq2_ref, lk2_ref, g_ref, o_ref,
                 acc_ref[0] / l_ref[0], acc_ref[1] / l_ref[1], lam_init)


def _attn(qn, ka, vb, bound, qtab, lq1, lk1, lq2, lk2, g_sub, lam_init):
    lp = qn.shape[0]
    lamspec = pl.BlockSpec((1, A_DH), lambda h, i: (0, 0))
    lams = [lamspec] * 4

    def common(tq):
        return dict(
            out_shape=jax.ShapeDtypeStruct((lp, A_HEADS * A_DV), BF16),
            grid=(A_HEADS, lp // tq),
            out_specs=pl.BlockSpec((tq, A_DV), lambda h, i: (i, h)),
            compiler_params=_params(("parallel", "arbitrary")),
        )

    def data_specs(tq):
        return [pl.BlockSpec((tq, LANES), lambda h, i: (i, h)),
                pl.BlockSpec((lp, 2 * LANES), lambda h, i: (0, h)),
                pl.BlockSpec((lp, LANES), lambda h, i: (0, h)),
                pl.BlockSpec((1, A_DV), lambda h, i: (0, 0))]

    def fast(qn, ka, vb):
        tq = ATTN_TQ
        return pl.pallas_call(
            functools.partial(_attn_fast_kernel, lam_init=lam_init),
            in_specs=(lams + [pl.BlockSpec((1, QTAB_ROWS, LANES), lambda h, i: (h, 0, 0))]
                      + data_specs(tq)),
            scratch_shapes=[pltpu.VMEM((2 * N_QVAR, tq, LANES), BF16),
                            pltpu.VMEM((2, tq, LANES), F32), pltpu.VMEM((2, tq, A_DV), F32)],
            name="attn_fast", **common(tq),
        )(lq1, lk1, lq2, lk2, qtab, qn, ka, vb, g_sub)

    def safe(qn, ka, vb):
        tq = SAFE_TQ
        return pl.pallas_call(
            functools.partial(_attn_safe_kernel, lam_init=lam_init),
            in_specs=lams + data_specs(tq),
            scratch_shapes=[pltpu.VMEM((2, tq, 1), F32), pltpu.VMEM((2, tq, 1), F32),
                            pltpu.VMEM((2, tq, A_DV), F32)],
            name="attn_safe", **common(tq),
        )(lq1, lk1, lq2, lk2, qn, ka, vb, g_sub)

    return lax.cond(bound <= FAST_EXP_BOUND, fast, safe, qn, ka, vb)


def _gdnprep_kernel(xq_ref, xk_ref, xv_ref, cw_ref, sm_ref, alog_ref, dtb_ref,
                    q_out, k_out, v_out, beta_out, gcum_out, gcumt_out, xbuf_ref):
    tm = xq_ref.shape[0]
    halo = SUBLANES

    @pl.when(pl.program_id(0) == 0)
    def _():
        xbuf_ref[0:halo, :] = jnp.zeros((halo, 3 * BLK), F32)

    xbuf_ref[halo:, 0:BLK] = xq_ref[...].astype(F32)
    xbuf_ref[halo:, BLK:2 * BLK] = xk_ref[...].astype(F32)
    xbuf_ref[halo:, 2 * BLK:] = xv_ref[...].astype(F32)

    outs = (q_out, k_out, v_out)
    for s in range(3 * BLK // LANES):
        sl = slice(s * LANES, (s + 1) * LANES)
        xw = xbuf_ref[:, sl]
        acc = xw[halo:] * cw_ref[DN_CONV - 1:DN_CONV, sl]
        for back in range(1, DN_CONV):
            shifted = pltpu.roll(xw, back, axis=0)[halo:]
            acc = acc + shifted * cw_ref[DN_CONV - 1 - back:DN_CONV - back, sl]
        y = _silu(acc)
        part, hh = divmod(s, DN_HEADS)
        if part < 2:
            y = y * lax.rsqrt(jnp.sum(y * y, axis=-1, keepdims=True) + L2_EPS)
            if part == 0:
                y = y * (DN_DK ** -0.5)
        outs[part][:, hh * LANES:(hh + 1) * LANES] = y.astype(BF16)

    xbuf_ref[0:halo, :] = xbuf_ref[tm:tm + halo, :]

    beta_out[...] = jax.nn.sigmoid(sm_ref[:, 0:LANES])
    g = -jnp.exp(alog_ref[...]) * jax.nn.softplus(sm_ref[:, LANES:] + dtb_ref[...])
    g1 = g.astype(BF16)
    r1 = g - g1.astype(F32)
    g2 = r1.astype(BF16)
    g3 = (r1 - g2.astype(F32)).astype(BF16)
    ii = lax.broadcasted_iota(jnp.int32, (GDN_CHUNK, GDN_CHUNK), 0)
    jj = lax.broadcasted_iota(jnp.int32, (GDN_CHUNK, GDN_CHUNK), 1)
    tril = (ii >= jj).astype(BF16)
    for cidx in range(tm // GDN_CHUNK):
        rs = slice(cidx * GDN_CHUNK, (cidx + 1) * GDN_CHUNK)
        gc = (jnp.dot(tril, g1[rs], preferred_element_type=F32)
              + jnp.dot(tril, g2[rs], preferred_element_type=F32)
              + jnp.dot(tril, g3[rs], preferred_element_type=F32))
        gcum_out[rs, :] = gc
        gcumt_out[:, rs] = gc.T[0:SUBLANES, :]


def _gdnprep(proj, small, conv_w, alog, dtb, tm):
    lp = proj.shape[0]
    blk = lambda j: pl.BlockSpec((tm, BLK), lambda m, j=j: (m, j))
    row = pl.BlockSpec((tm, LANES), lambda m: (m, 0))
    vec = pl.BlockSpec((1, LANES), lambda m: (0, 0))
    o16 = jax.ShapeDtypeStruct((lp, BLK), BF16)
    o32 = jax.ShapeDtypeStruct((lp, LANES), F32)
    return pl.pallas_call(
        _gdnprep_kernel,
        out_shape=(o16, o16, o16, o32, o32, jax.ShapeDtypeStruct((SUBLANES, lp), F32)),
        grid=(lp // tm,),
        in_specs=[blk(3), blk(4), blk(5),
                  pl.BlockSpec((DN_CONV, 3 * BLK), lambda m: (0, 0)),
                  pl.BlockSpec((tm, 2 * LANES), lambda m: (m, 0)), vec, vec],
        out_specs=(blk(0), blk(0), blk(0), row, row,
                   pl.BlockSpec((SUBLANES, tm), lambda m: (0, m))),
        scratch_shapes=[pltpu.VMEM((tm + SUBLANES, 3 * BLK), F32)],
        compiler_params=_params(("arbitrary",)),
        name="gdnprep",
    )(proj, proj, proj, conv_w, small, alog, dtb)


def _gdn_kernel(q_ref, k_ref, v_ref, beta_ref, gc_ref, gt_ref, z_ref, gn_ref, o_ref, s_ref):
    c = GDN_CHUNK
    heads = range(DN_HEADS)

    @pl.when(pl.program_id(0) == 0)
    def _():
        s_ref[...] = jnp.zeros(s_ref.shape, F32)

    ii = lax.broadcasted_iota(jnp.int32, (c, c), 0)
    jj = lax.broadcasted_iota(jnp.int32, (c, c), 1)
    ge = ii >= jj
    gt = ii > jj
    eye = (ii == jj).astype(F32)
    blk_id = lambda t: (jnp.right_shift(ii, t), jnp.right_shift(jj, t))
    same = lambda t: blk_id(t)[0] == blk_id(t)[1]
    pair_mask = same(1)
    merge_masks = [same(t + 1) & jnp.logical_not(same(t)) for t in range(1, int(math.log2(c)))]

    def mm(a, b):
        return jnp.dot(a.astype(BF16), b.astype(BF16), preferred_element_type=F32)

    units = [(ci, hh) for ci in range(GDN_CHUNKS_PER_STEP) for hh in heads]
    rows = lambda u: slice(u[0] * c, (u[0] + 1) * c)
    cols = lambda u: slice(u[1] * LANES, (u[1] + 1) * LANES)
    q16 = {u: q_ref[rows(u), cols(u)] for u in units}
    k16 = {u: k_ref[rows(u), cols(u)] for u in units}
    k = {u: k16[u].astype(F32) for u in units}
    beta = {u: beta_ref[rows(u), u[1]:u[1] + 1] for u in units}
    g_col = {u: gc_ref[rows(u), u[1]:u[1] + 1] for u in units}
    g_last = {u: gc_ref[(u[0] + 1) * c - 1:(u[0] + 1) * c, u[1]:u[1] + 1] for u in units}
    decay = {u: jnp.exp(jnp.where(ge, g_col[u] - gt_ref[u[1]:u[1] + 1, rows(u)], -jnp.inf))
             for u in units}
    e_g = {u: jnp.exp(g_col[u]) for u in units}
    kb = {u: k[u] * beta[u] for u in units}
    vb = {u: v_ref[rows(u), cols(u)].astype(F32) * beta[u] for u in units}
    kq = {u: lax.dot_general(jnp.concatenate([kb[u].astype(BF16), q16[u]], axis=0), k16[u],
                             NT_DIMS, preferred_element_type=F32) for u in units}
    a = {u: jnp.where(gt, kq[u][:c] * decay[u], 0.0) for u in units}
    qk = {u: jnp.where(ge, kq[u][c:] * decay[u], 0.0) for u in units}
    x = {u: eye - jnp.where(pair_mask, a[u], 0.0) for u in units}
    for mask in merge_masks:
        y = {u: mm(jnp.where(mask, a[u], 0.0), x[u]) for u in units}
        x = {u: x[u] - mm(x[u], y[u]) for u in units}
    uw = {u: mm(x[u], jnp.concatenate([vb[u], kb[u] * e_g[u]], axis=1)) for u in units}
    wq = {u: jnp.concatenate([uw[u][:, DN_DV:], q16[u].astype(F32) * e_g[u]], axis=0) for u in units}
    kd = {u: (k[u] * jnp.exp(g_last[u] - g_col[u])).astype(BF16) for u in units}
    state = [s_ref[hh] for hh in heads]
    for ci in range(GDN_CHUNKS_PER_STEP):
        us = [(ci, hh) for hh in heads]
        ws = [mm(wq[u], state[u[1]]) for u in us]
        v_new = [uw[u][:, :DN_DV] - ws[u[1]][:c] for u in us]
        o = [ws[u[1]][c:] + mm(qk[u], v_new[u[1]]) for u in us]
        state = [state[u[1]] * jnp.exp(g_last[u]) + lax.dot_general(
            kd[u], v_new[u[1]].astype(BF16), TN_DIMS, preferred_element_type=F32) for u in us]
        for u in us:
            ou = o[u[1]]
            ms = jnp.mean(ou * ou, axis=-1, keepdims=True)
            y = (((ou * lax.rsqrt(ms + RMS_EPS)) * gn_ref[...])
                 * _silu(z_ref[rows(u), cols(u)].astype(F32)))
            o_ref[rows(u), cols(u)] = y.astype(BF16)
    for hh in heads:
        s_ref[hh] = state[hh]


def _gdn(qn, kn, vn, beta, gcum, gcum_t, proj, gn):
    lp = qn.shape[0]
    r = GDN_CHUNK * GDN_CHUNKS_PER_STEP
    blk = lambda j: pl.BlockSpec((r, BLK), lambda n, j=j: (n, j))
    row = pl.BlockSpec((r, LANES), lambda n: (n, 0))
    return pl.pallas_call(
        _gdn_kernel,
        out_shape=jax.ShapeDtypeStruct((lp, DN_HEADS * DN_DV), BF16),
        grid=(lp // r,),
        in_specs=[blk(0), blk(0), blk(0), row, row,
                  pl.BlockSpec((SUBLANES, r), lambda n: (0, n)),
                  blk(6), pl.BlockSpec((1, DN_DV), lambda n: (0, 0))],
        out_specs=blk(0),
        scratch_shapes=[pltpu.VMEM((DN_HEADS, DN_DK, DN_DV), F32)],
        compiler_params=_params(("arbitrary",)),
        name="gdn",
    )(qn, kn, vn, beta, gcum, gcum_t, proj, gn)


def _mix_kernel(ao_ref, do_ref, ga_ref, gb_ref, h_ref, wa_ref, wb_ref, wo_ref, o_ref):
    ya = jnp.dot(ao_ref[...], wa_ref[...], preferred_element_type=F32)
    yb = jnp.dot(do_ref[...], wb_ref[...], preferred_element_type=F32)
    mixed = (jax.nn.sigmoid(ga_ref[...].astype(F32)) * ya
             + jax.nn.sigmoid(gb_ref[...].astype(F32)) * yb)
    o_ref[...] = h_ref[...] + jnp.dot(mixed.astype(BF16), wo_ref[...],
                                      preferred_element_type=F32)


def _mix(ao, do, proj, h, wa, wb, wo, tm):
    lp = h.shape[0]
    blk = lambda j: pl.BlockSpec((tm, BLK), lambda m, j=j: (m, j))
    wspec = pl.BlockSpec((BLK, D_MODEL), lambda m: (0, 0))
    return pl.pallas_call(
        _mix_kernel,
        out_shape=jax.ShapeDtypeStruct((lp, D_MODEL), F32),
        grid=(lp // tm,),
        in_specs=[blk(0), blk(0), blk(7), blk(8), blk(0), wspec, wspec, wspec],
        out_specs=blk(0),
        compiler_params=_params(("parallel",)),
        name="mix",
    )(ao, do, proj, proj, h, wa, wb, wo)


def _ffn_kernel(h_ref, g_ref, wup_ref, cw_ref, wdn_ref, o_ref, fbuf_ref, *, tc):
    tm = h_ref.shape[0]
    halo = SUBLANES

    @pl.when(pl.program_id(0) == 0)
    def _():
        fbuf_ref[0:halo, :] = jnp.zeros((halo, 2 * D_FF), F32)

    @pl.when(pl.program_id(0) > 0)
    def _():
        fbuf_ref[0:halo, :] = fbuf_ref[tm:tm + halo, :]

    x = h_ref[...]
    ms = jnp.mean(x * x, axis=-1, keepdims=True)
    u = ((x * lax.rsqrt(ms + RMS_EPS)) * g_ref[...]).astype(BF16)
    for j in range(2 * D_FF // tc):
        sl = slice(j * tc, (j + 1) * tc)
        fbuf_ref[halo:, sl] = jnp.dot(u, wup_ref[:, sl], preferred_element_type=F32)

    def conv(sl):
        acc = None
        for t in range(FFN_CONV):
            off = halo - (FFN_CONV - 1) + t
            term = fbuf_ref[off:off + tm, sl] * cw_ref[t:t + 1, sl]
            acc = term if acc is None else acc + term
        return acc

    acc = x
    for j in range(D_FF // tc):
        gate = conv(slice(j * tc, (j + 1) * tc))
        up = conv(slice(D_FF + j * tc, D_FF + (j + 1) * tc))
        act = (_silu(gate) * up).astype(BF16)
        acc = acc + jnp.dot(act, wdn_ref[j * tc:(j + 1) * tc, :], preferred_element_type=F32)
    o_ref[...] = acc


def _ffn(h, g, wup, cw, wdn, tm, tc=256):
    lp = h.shape[0]
    once = pl.Buffered(1)
    return pl.pallas_call(
        functools.partial(_ffn_kernel, tc=tc),
        out_shape=jax.ShapeDtypeStruct((lp, D_MODEL), F32),
        grid=(lp // tm,),
        in_specs=[pl.BlockSpec((tm, D_MODEL), lambda m: (m, 0)),
                  pl.BlockSpec((1, D_MODEL), lambda m: (0, 0)),
                  pl.BlockSpec((D_MODEL, 2 * D_FF), lambda m: (0, 0), pipeline_mode=once),
                  pl.BlockSpec((FFN_CONV, 2 * D_FF), lambda m: (0, 0)),
                  pl.BlockSpec((D_FF, D_MODEL), lambda m: (0, 0), pipeline_mode=once)],
        out_specs=pl.BlockSpec((tm, D_MODEL), lambda m: (m, 0)),
        scratch_shapes=[pltpu.VMEM((tm + SUBLANES, 2 * D_FF), F32)],
        compiler_params=_params(("arbitrary",)),
        name="ffn",
    )(h, g, wup, cw, wdn)


def _regroup_w_in(w):
    a_cols = 3 * A_HEADS * A_DV
    dn_cols = DN_HEADS * (2 * DN_DK + DN_DV) + DN_HEADS * DN_DV
    o_small = a_cols + dn_cols
    w_main = jnp.concatenate([w[:, :o_small], w[:, o_small + 2 * DN_HEADS:]], axis=1)
    w_small = jnp.zeros((D_MODEL, 2 * LANES), w.dtype)
    w_small = w_small.at[:, 0:DN_HEADS].set(w[:, o_small:o_small + DN_HEADS])
    w_small = w_small.at[:, LANES:LANES + DN_HEADS].set(
        w[:, o_small + DN_HEADS:o_small + 2 * DN_HEADS])
    return w_main.astype(BF16), w_small.astype(BF16)


def _pad_lanes(v):
    return jnp.zeros((1, LANES), F32).at[0, :v.shape[0]].set(v.astype(F32))


def kernel(x, meta_tokens, mix_norm_g, w_in, q_norm_g, k_norm_g, lambda_q1, lambda_k1, lambda_q2, lambda_k2, attn_subln_g, dn_conv_w, dn_a_log, dn_dt_bias, dn_norm_g, w_branch_attn, w_branch_dn, w_out, ffn_norm_g, w_ffn_up, ffn_conv_w, w_ffn_down):
    batch, seq, _ = x.shape
    assert batch == 1
    depth = w_in.shape[0]
    length = N_META + seq
    lp = -(-length // ROW_ALIGN) * ROW_ALIGN
    tm = _pick_tile(lp, (768, 512))
    tm_ffn = _pick_tile(lp, (384, 256))
    qtab, ktab = _alibi_tables()
    h = jnp.concatenate([meta_tokens.astype(F32), x[0],
                         jnp.zeros((lp - length, D_MODEL), F32)], axis=0)
    row = lambda v: v.astype(F32)[None, :]
    for layer in range(depth):
        lam_init = 0.8 - 0.6 * math.exp(-0.3 * layer)
        w_main, w_small = _regroup_w_in(w_in[layer])
        proj, small = _inproj(h, row(mix_norm_g[layer]), w_main, w_small, tm)

        gq2 = jnp.tile(row(q_norm_g[layer]), (1, 2))
        gk2 = jnp.tile(row(k_norm_g[layer]), (1, 2))
        qn, ka, vb = _qkprep(proj, gq2, gk2, ktab, tm)
        bound = (A_DH ** 0.5) * LOG2E * jnp.max(jnp.abs(gq2)) * jnp.max(jnp.abs(gk2))
        ao = _attn(qn, ka, vb, bound, qtab, row(lambda_q1[layer]), row(lambda_k1[layer]),
                   row(lambda_q2[layer]), row(lambda_k2[layer]), row(attn_subln_g[layer]),
                   lam_init)

        dq, dk, dv, beta, gcum, gcum_t = _gdnprep(
            proj, small, dn_conv_w[layer].astype(F32), _pad_lanes(dn_a_log[layer]),
            _pad_lanes(dn_dt_bias[layer]), tm)
        do = _gdn(dq, dk, dv, beta, gcum, gcum_t, proj, row(dn_norm_g[layer]))

        h = _mix(ao, do, proj, h, w_branch_attn[layer].astype(BF16),
                 w_branch_dn[layer].astype(BF16), w_out[layer].astype(BF16), tm)
        h = _ffn(h, row(ffn_norm_g[layer]), w_ffn_up[layer].astype(BF16),
                 ffn_conv_w[layer].astype(F32), w_ffn_down[layer].astype(BF16), tm_ffn)
    return h[N_META:N_META + seq][None]
```

```python
import functools
import math

import jax
import jax.numpy as jnp
import ml_dtypes
import numpy as np
from jax import lax
from jax.experimental import pallas as pl
from jax.experimental.pallas import tpu as pltpu

D_MODEL = 1024
N_META = 16
A_HEADS = 8
A_DH = 64
A_DV = 2 * A_DH
DN_HEADS = 8
DN_DK = 128
DN_DV = 128
DN_CONV = 4
D_FF = 2816
FFN_CONV = 3
RMS_EPS = 1e-6
L2_EPS = 1e-6
ALIBI_MAX_EXP = 8.0

LANES = 128
SUBLANES = 8
ROW_ALIGN = 512
ATTN_TQ = 512
ATTN_TK = 256
ATTN_GROUP = 8
ATTN_HALF_GROUP = 4
SAFE_TQ = 256
SAFE_TK_BIG = 1024
GDN_CHUNK = 128
GDN_CHUNKS_PER_STEP = 2
BLK = 1024
LOG2E = 1.4426950408889634
VMEM_LIMIT = 52 * 1024 * 1024

FAST_EXP_BOUND = 60.0
ZERO_SCALE_EXP = 150.0

AUG_ROWB = 0
AUG_ROWA = 3
AUG_JR = 6
AUG_ONE = 9
AUG_JREL = 12
N_PIECES = 3
assert ATTN_TQ // ATTN_TK <= 2 and ATTN_TK == 256
AUG_BASE = (A_DH, 0)
N_QVAR = 1 + ATTN_TQ // ATTN_TK
QTAB_SLOPE_ROW = 2 * N_QVAR
QTAB_ROWS = 8

F32 = jnp.float32
BF16 = jnp.bfloat16
NT_DIMS = (((1,), (1,)), ((), ()))
TN_DIMS = (((0,), (0,)), ((), ()))


def _params(sem, vmem=VMEM_LIMIT):
    return pltpu.CompilerParams(dimension_semantics=sem, vmem_limit_bytes=vmem)


def _pick_tile(n, candidates):
    for c in candidates:
        if n % c == 0:
            return c
    raise ValueError(f"no tile in {candidates} divides {n}")


def _silu(x):
    return x * jax.nn.sigmoid(x)


def _bf16_pieces(x):
    out, r = [], np.float64(x)
    for _ in range(N_PIECES):
        p = float(np.float32(r).astype(ml_dtypes.bfloat16).astype(np.float32))
        out.append(p)
        r -= p
    return out


def _alibi_tables():
    qtab = np.zeros((A_HEADS, QTAB_ROWS, LANES), np.float32)
    ktab = np.zeros((2 * A_HEADS, LANES), np.float32)
    for h in range(A_HEADS):
        slope = 2.0 ** (-(ALIBI_MAX_EXP / A_HEADS) * (h + 1))
        pieces = _bf16_pieces(slope * LOG2E)
        qtab[h, QTAB_SLOPE_ROW, :] = np.float32(sum(pieces))
        for c in range(2):
            b = AUG_BASE[c]
            for i, p in enumerate(pieces):
                ktab[2 * h + c, b + AUG_ROWB + i] = -p
                ktab[2 * h + c, b + AUG_ROWA + i] = -float(ATTN_TK) * p
                ktab[2 * h + c, b + AUG_ONE + i] = 1.0
                qtab[h, N_QVAR * c, b + AUG_JR + i] = -p
                for d in range(ATTN_TQ // ATTN_TK):
                    qtab[h, N_QVAR * c + 1 + d, b + AUG_JREL + i] = p
                    qtab[h, N_QVAR * c + 1 + d, b + AUG_ONE + i] = p * ATTN_TK * d
    return jnp.asarray(qtab), jnp.asarray(ktab)


def _inproj_kernel(h_ref, g_ref, w_ref, ws_ref, o_ref, os_ref, u_ref):
    @pl.when(pl.program_id(1) == 0)
    def _():
        x = h_ref[...]
        ms = jnp.mean(x * x, axis=-1, keepdims=True)
        u = ((x * lax.rsqrt(ms + RMS_EPS)) * g_ref[...]).astype(BF16)
        u_ref[...] = u
        os_ref[...] = jnp.dot(u, ws_ref[...], preferred_element_type=F32)

    o_ref[...] = jnp.dot(u_ref[...], w_ref[...], preferred_element_type=F32).astype(o_ref.dtype)


def _inproj(h, g, w_main, w_small, tm, tn=BLK):
    lp = h.shape[0]
    n_main = w_main.shape[1]
    n_small = w_small.shape[1]
    return pl.pallas_call(
        _inproj_kernel,
        out_shape=(jax.ShapeDtypeStruct((lp, n_main), BF16),
                   jax.ShapeDtypeStruct((lp, n_small), F32)),
        grid=(lp // tm, n_main // tn),
        in_specs=[
            pl.BlockSpec((tm, D_MODEL), lambda m, n: (m, 0)),
            pl.BlockSpec((1, D_MODEL), lambda m, n: (0, 0)),
            pl.BlockSpec((D_MODEL, tn), lambda m, n: (0, n)),
            pl.BlockSpec((D_MODEL, n_small), lambda m, n: (0, 0)),
        ],
        out_specs=(
            pl.BlockSpec((tm, tn), lambda m, n: (m, n)),
            pl.BlockSpec((tm, n_small), lambda m, n: (m, 0)),
        ),
        scratch_shapes=[pltpu.VMEM((tm, D_MODEL), BF16)],
        compiler_params=_params(("parallel", "arbitrary")),
        name="inproj",
    )(h, g, w_main, w_small)


def _qkprep_kernel(aq_ref, ak_ref, av_ref, gq_ref, gk_ref, ktab_ref, q_out, k_out, v_out):
    tm = aq_ref.shape[0]
    lane = lax.broadcasted_iota(jnp.int32, (tm, LANES), 1)
    lo = lane < A_DH
    q_scale = (A_DH ** -0.5) * LOG2E

    def norm_pair(x, g):
        ss = x * x
        s_lo = jnp.sum(jnp.where(lo, ss, 0.0), axis=-1, keepdims=True)
        s_hi = jnp.sum(jnp.where(lo, 0.0, ss), axis=-1, keepdims=True)
        ms = jnp.where(lo, s_lo, s_hi) * (1.0 / A_DH)
        return (x * lax.rsqrt(ms + RMS_EPS)) * g

    j = (pl.program_id(0) * tm + lax.broadcasted_iota(jnp.int32, (tm, LANES), 0))
    jrel = jnp.bitwise_and(j, ATTN_TK - 1)
    jr = (ATTN_TK - 1) - jrel
    pos = []
    for c in range(2):
        la = lane - AUG_BASE[c]
        pick = lambda off, la=la: (la >= off) & (la < off + N_PIECES)
        pos.append(jnp.where(pick(AUG_JR), jr, jnp.where(pick(AUG_JREL), jrel, 0)).astype(F32))

    for hh in range(A_HEADS):
        sl = slice(hh * LANES, (hh + 1) * LANES)
        yq = norm_pair(aq_ref[:, sl].astype(F32), gq_ref[...]) * q_scale
        yk = norm_pair(ak_ref[:, sl].astype(F32), gk_ref[...])
        q_out[:, sl] = yq.astype(BF16)
        k_out[:, 2 * hh * LANES:(2 * hh + 1) * LANES] = jnp.where(
            lo, yk, pos[0] + ktab_ref[2 * hh:2 * hh + 1, :]).astype(BF16)
        k_out[:, (2 * hh + 1) * LANES:(2 * hh + 2) * LANES] = jnp.where(
            lo, pos[1] + ktab_ref[2 * hh + 1:2 * hh + 2, :], yk).astype(BF16)
    v_out[...] = av_ref[...]


def _qkprep(proj, gq2, gk2, ktab, tm):
    lp = proj.shape[0]
    blk = lambda j: pl.BlockSpec((tm, BLK), lambda m, j=j: (m, j))
    vec = pl.BlockSpec((1, LANES), lambda m: (0, 0))
    out = jax.ShapeDtypeStruct((lp, BLK), BF16)
    return pl.pallas_call(
        _qkprep_kernel,
        out_shape=(out, jax.ShapeDtypeStruct((lp, 2 * BLK), BF16), out),
        grid=(lp // tm,),
        in_specs=[blk(0), blk(1), blk(2), vec, vec,
                  pl.BlockSpec((2 * A_HEADS, LANES), lambda m: (0, 0))],
        out_specs=(blk(0), pl.BlockSpec((tm, 2 * BLK), lambda m: (m, 0)), blk(0)),
        compiler_params=_params(("parallel",)),
        name="qkprep",
    )(proj, proj, proj, gq2, gk2, ktab)


def _attn_finish(lq1_ref, lk1_ref, lq2_ref, lk2_ref, g_ref, o_ref, o1, o2, lam_init):
    lam = (jnp.exp(jnp.sum(lq1_ref[...] * lk1_ref[...], axis=-1, keepdims=True))
           - jnp.exp(jnp.sum(lq2_ref[...] * lk2_ref[...], axis=-1, keepdims=True))
           + lam_init)
    o = o1 - lam * o2
    ms = jnp.mean(o * o, axis=-1, keepdims=True)
    y = ((o * lax.rsqrt(ms + RMS_EPS)) * g_ref[...]) * (1.0 - lam_init)
    o_ref[...] = y.astype(BF16)


def _attn_fast_kernel(lq1_ref, lk1_ref, lq2_ref, lk2_ref, qtab_ref, q_ref, k_ref, v_ref, g_ref,
                      o_ref, qa_ref, l_ref, acc_ref, *, lam_init):
    tq, tk, grp = ATTN_TQ, ATTN_TK, ATTN_GROUP
    n_diag = tq // tk
    log_tk = int(math.log2(tk))
    h = pl.program_id(0)
    i = pl.program_id(1)
    row0 = i * tq
    tab = qtab_ref[0]
    slope2 = tab[QTAB_SLOPE_ROW:QTAB_SLOPE_ROW + 1, :]
    q = q_ref[...].astype(F32)
    lane = lax.broadcasted_iota(jnp.int32, (tq, LANES), 1)
    rowi = lax.broadcasted_iota(jnp.int32, (tq, LANES), 0)
    row_b = jnp.bitwise_and(rowi, tk - 1).astype(F32)
    row_a = jnp.right_shift(rowi, log_tk).astype(F32)
    tab16 = tab.astype(BF16)
    for c in range(2):
        content = (lane < A_DH) if c == 0 else (lane >= A_DH)
        la = lane - AUG_BASE[c]
        pick = lambda off, la=la: (la >= off) & (la < off + N_PIECES)
        rows = jnp.where(pick(AUG_ROWB), row_b, jnp.where(pick(AUG_ROWA), row_a, 0.0))
        base = jnp.where(content, q, rows).astype(BF16)
        for variant in range(N_QVAR):
            qa_ref[N_QVAR * c + variant] = base + tab16[N_QVAR * c + variant:N_QVAR * c + variant + 1, :]

    def scores(c, variant, start, rows=slice(None)):
        s = lax.dot_general(qa_ref[N_QVAR * c + variant, rows, :],
                            k_ref[pl.ds(start, tk), c * LANES:(c + 1) * LANES],
                            NT_DIMS, preferred_element_type=F32)
        return s

    slope2_tile = jnp.concatenate([slope2] * (tk // LANES), axis=1)

    def far_tiles(start, n_tiles, scale_dist):
        scale = jnp.exp2(-slope2 * scale_dist.astype(F32))
        for c in range(2):
            pv = None
            lsum = None
            for t in range(n_tiles):
                s = scores(c, 0, start + t * tk)
                after = n_tiles - 1 - t
                if after:
                    s = s - slope2_tile * float(tk * after)
                p = jnp.exp2(s)
                part = p[:, :LANES] + p[:, LANES:]
                lsum = part if lsum is None else lsum + part
                d = jnp.dot(p.astype(BF16), v_ref[pl.ds(start + t * tk, tk), :],
                            preferred_element_type=F32)
                pv = d if pv is None else pv + d
            acc_ref[c] += pv * scale
            l_ref[c] += lsum * scale

    l_ref[...] = jnp.zeros(l_ref.shape, F32)
    acc_ref[...] = jnp.zeros(acc_ref.shape, F32)

    assert ALIBI_MAX_EXP == A_HEADS
    zero_dist = jnp.left_shift(jnp.int32(math.ceil(ZERO_SCALE_EXP / LOG2E * 2.0)), h)
    avail = jnp.right_shift(row0, log_tk)
    first_tile = jnp.right_shift(jnp.maximum(row0 - zero_dist, 0), log_tk)
    need = avail - first_tile
    half = ATTN_HALF_GROUP
    n_groups = jnp.minimum((need + grp - half - 1) // grp, avail // grp)
    group_tile0 = avail - n_groups * grp
    left = jnp.maximum(group_tile0 - first_tile, 0)
    n_half = jnp.where((left > 0) & (group_tile0 >= half), 1, 0)
    half_tile0 = group_tile0 - n_half * half
    n_single = jnp.maximum(half_tile0 - first_tile, 0)

    def single_body(js, carry):
        start = pl.multiple_of((first_tile + js) * tk, tk)
        far_tiles(start, 1, row0 - (start + tk - 1))
        return carry

    lax.fori_loop(0, n_single, single_body, 0)

    def group_body(jg, carry):
        start = pl.multiple_of((group_tile0 + jg * grp) * tk, tk)
        far_tiles(start, grp, row0 - (start + grp * tk - 1))
        return carry

    lax.fori_loop(0, n_groups, group_body, 0)

    def diag_tiles():
        for d in range(n_diag):
            rows = slice(d * tk, tq)
            n_rows = tq - d * tk
            keep = (lax.broadcasted_iota(jnp.int32, (1, tk), 1)
                    <= lax.broadcasted_iota(jnp.int32, (n_rows, 1), 0))
            start = pl.multiple_of(row0 + d * tk, tk)
            for c in range(2):
                s = jnp.where(keep, scores(c, 1 + d, start, rows), -jnp.inf)
                p = jnp.exp2(s)
                acc_ref[c, rows, :] += jnp.dot(p.astype(BF16), v_ref[pl.ds(start, tk), :],
                                               preferred_element_type=F32)
                l_ref[c, rows, :] += p[:, :LANES] + p[:, LANES:]

    def finish():
        o1 = acc_ref[0] / jnp.sum(l_ref[0], axis=-1, keepdims=True)
        o2 = acc_ref[1] / jnp.sum(l_ref[1], axis=-1, keepdims=True)
        _attn_finish(lq1_ref, lk1_ref, lq2_ref, lk2_ref, g_ref, o_ref, o1, o2, lam_init)

    @pl.when(n_half == 1)
    def _():
        start = pl.multiple_of(half_tile0 * tk, tk)
        far_tiles(start, half, row0 - (start + half * tk - 1))
        diag_tiles()
        finish()

    @pl.when(n_half == 0)
    def _():
        diag_tiles()
        finish()


def _attn_safe_kernel(lq1_ref, lk1_ref, lq2_ref, lk2_ref, q_ref, k_ref, v_ref, g_ref, o_ref,
                      m_ref, l_ref, acc_ref, *, lam_init):
    tq, tb = SAFE_TQ, SAFE_TK_BIG
    h = pl.program_id(0)
    i = pl.program_id(1)
    slope2 = jnp.exp2(-(ALIBI_MAX_EXP / A_HEADS) * (h + 1).astype(F32)) * LOG2E
    q = q_ref[...].astype(F32)
    lane = lax.broadcasted_iota(jnp.int32, (tq, LANES), 1)
    q_maps = (jnp.where(lane < A_DH, q, 0.0).astype(BF16), jnp.where(lane >= A_DH, q, 0.0).astype(BF16))
    m_ref[...] = jnp.full(m_ref.shape, -jnp.inf, F32)
    l_ref[...] = jnp.zeros(l_ref.shape, F32)
    acc_ref[...] = jnp.zeros(acc_ref.shape, F32)
    row0 = i * tq

    def block(start, size, masked):
        v = v_ref[pl.ds(start, size), :]
        col = lax.broadcasted_iota(jnp.int32, (1, size), 1) + (start - row0)
        bias = slope2 * col.astype(F32)
        if masked:
            keep = col <= lax.broadcasted_iota(jnp.int32, (tq, 1), 0)
        for c in range(2):
            k = k_ref[pl.ds(start, size), c * LANES:(c + 1) * LANES]
            s = lax.dot_general(q_maps[c], k, NT_DIMS, preferred_element_type=F32) + bias
            if masked:
                s = jnp.where(keep, s, -jnp.inf)
            m_prev = m_ref[c]
            m_new = jnp.maximum(m_prev, jnp.max(s, axis=-1, keepdims=True))
            alpha = jnp.exp2(m_prev - m_new)
            p = jnp.exp2(s - m_new)
            l_ref[c] = alpha * l_ref[c] + jnp.sum(p, axis=-1, keepdims=True)
            acc_ref[c] = alpha * acc_ref[c] + jnp.dot(p.astype(BF16), v,
                                                      preferred_element_type=F32)
            m_ref[c] = m_new

    n_big = row0 // tb

    def big_body(jb, carry):
        block(pl.multiple_of(jb * tb, tb), tb, False)
        return carry

    lax.fori_loop(0, n_big, big_body, 0)
    base = n_big * tb
    n_small = (row0 - base) // tq

    def small_body(js, carry):
        block(pl.multiple_of(base + js * tq, tq), tq, False)
        return carry

    lax.fori_loop(0, n_small, small_body, 0)
    block(pl.multiple_of(row0, tq), tq, True)
    _attn_finish(lq1_ref, lk1_ref, lq2_ref, lk2_ref, g_ref, o_ref,
                 acc_ref[0] / l_ref[0], acc_ref[1] / l_ref[1], lam_init)


def _attn(qn, ka, vb, bound, qtab, lq1, lk1, lq2, lk2, g_sub, lam_init):
    lp = qn.shape[0]
    lamspec = pl.BlockSpec((1, A_DH), lambda h, i: (0, 0))
    lams = [lamspec] * 4

    def common(tq):
        return dict(
            out_shape=jax.ShapeDtypeStruct((lp, A_HEADS * A_DV), BF16),
            grid=(A_HEADS, lp // tq),
            out_specs=pl.BlockSpec((tq, A_DV), lambda h, i: (i, h)),
            compiler_params=_params(("parallel", "arbitrary")),
        )

    def data_specs(tq):
        return [pl.BlockSpec((tq, LANES), lambda h, i: (i, h)),
                pl.BlockSpec((lp, 2 * LANES), lambda h, i: (0, h)),
                pl.BlockSpec((lp, LANES), lambda h, i: (0, h)),
                pl.BlockSpec((1, A_DV), lambda h, i: (0, 0))]

    def fast(qn, ka, vb):
        tq = ATTN_TQ
        return pl.pallas_call(
            functools.partial(_attn_fast_kernel, lam_init=lam_init),
            in_specs=(lams + [pl.BlockSpec((1, QTAB_ROWS, LANES), lambda h, i: (h, 0, 0))]
                      + data_specs(tq)),
            scratch_shapes=[pltpu.VMEM((2 * N_QVAR, tq, LANES), BF16),
                            pltpu.VMEM((2, tq, LANES), F32), pltpu.VMEM((2, tq, A_DV), F32)],
            name="attn_fast", **common(tq),
        )(lq1, lk1, lq2, lk2, qtab, qn, ka, vb, g_sub)

    def safe(qn, ka, vb):
        tq = SAFE_TQ
        return pl.pallas_call(
            functools.partial(_attn_safe_kernel, lam_init=lam_init),
            in_specs=lams + data_specs(tq),
            scratch_shapes=[pltpu.VMEM((2, tq, 1), F32), pltpu.VMEM((2, tq, 1), F32),
                            pltpu.VMEM((2, tq, A_DV), F32)],
            name="attn_safe", **common(tq),
        )(lq1, lk1, lq2, lk2, qn, ka, vb, g_sub)

    return lax.cond(bound <= FAST_EXP_BOUND, fast, safe, qn, ka, vb)


def _gdnprep_kernel(xq_ref, xk_ref, xv_ref, cw_ref, sm_ref, alog_ref, dtb_ref,
                    q_out, k_out, v_out, beta_out, gcum_out, gcumt_out, xbuf_ref):
    tm = xq_ref.shape[0]
    halo = SUBLANES

    @pl.when(pl.program_id(0) == 0)
    def _():
        xbuf_ref[0:halo, :] = jnp.zeros((halo, 3 * BLK), F32)

    xbuf_ref[halo:, 0:BLK] = xq_ref[...].astype(F32)
    xbuf_ref[halo:, BLK:2 * BLK] = xk_ref[...].astype(F32)
    xbuf_ref[halo:, 2 * BLK:] = xv_ref[...].astype(F32)

    outs = (q_out, k_out, v_out)
    for s in range(3 * BLK // LANES):
        sl = slice(s * LANES, (s + 1) * LANES)
        xw = xbuf_ref[:, sl]
        acc = xw[halo:] * cw_ref[DN_CONV - 1:DN_CONV, sl]
        for back in range(1, DN_CONV):
            shifted = pltpu.roll(xw, back, axis=0)[halo:]
            acc = acc + shifted * cw_ref[DN_CONV - 1 - back:DN_CONV - back, sl]
        y = _silu(acc)
        part, hh = divmod(s, DN_HEADS)
        if part < 2:
            y = y * lax.rsqrt(jnp.sum(y * y, axis=-1, keepdims=True) + L2_EPS)
            if part == 0:
                y = y * (DN_DK ** -0.5)
        outs[part][:, hh * LANES:(hh + 1) * LANES] = y.astype(BF16)

    xbuf_ref[0:halo, :] = xbuf_ref[tm:tm + halo, :]

    beta_out[...] = jax.nn.sigmoid(sm_ref[:, 0:LANES])
    g = -jnp.exp(alog_ref[...]) * jax.nn.softplus(sm_ref[:, LANES:] + dtb_ref[...])
    g1 = g.astype(BF16)
    r1 = g - g1.astype(F32)
    g2 = r1.astype(BF16)
    g3 = (r1 - g2.astype(F32)).astype(BF16)
    ii = lax.broadcasted_iota(jnp.int32, (GDN_CHUNK, GDN_CHUNK), 0)
    jj = lax.broadcasted_iota(jnp.int32, (GDN_CHUNK, GDN_CHUNK), 1)
    tril = (ii >= jj).astype(BF16)
    for cidx in range(tm // GDN_CHUNK):
        rs = slice(cidx * GDN_CHUNK, (cidx + 1) * GDN_CHUNK)
        gc = (jnp.dot(tril, g1[rs], preferred_element_type=F32)
              + jnp.dot(tril, g2[rs], preferred_element_type=F32)
              + jnp.dot(tril, g3[rs], preferred_element_type=F32))
        gcum_out[rs, :] = gc
        gcumt_out[:, rs] = gc.T[0:SUBLANES, :]


def _gdnprep(proj, small, conv_w, alog, dtb, tm):
    lp = proj.shape[0]
    blk = lambda j: pl.BlockSpec((tm, BLK), lambda m, j=j: (m, j))
    row = pl.BlockSpec((tm, LANES), lambda m: (m, 0))
    vec = pl.BlockSpec((1, LANES), lambda m: (0, 0))
    o16 = jax.ShapeDtypeStruct((lp, BLK), BF16)
    o32 = jax.ShapeDtypeStruct((lp, LANES), F32)
    return pl.pallas_call(
        _gdnprep_kernel,
        out_shape=(o16, o16, o16, o32, o32, jax.ShapeDtypeStruct((SUBLANES, lp), F32)),
        grid=(lp // tm,),
        in_specs=[blk(3), blk(4), blk(5),
                  pl.BlockSpec((DN_CONV, 3 * BLK), lambda m: (0, 0)),
                  pl.BlockSpec((tm, 2 * LANES), lambda m: (m, 0)), vec, vec],
        out_specs=(blk(0), blk(0), blk(0), row, row,
                   pl.BlockSpec((SUBLANES, tm), lambda m: (0, m))),
        scratch_shapes=[pltpu.VMEM((tm + SUBLANES, 3 * BLK), F32)],
        compiler_params=_params(("arbitrary",)),
        name="gdnprep",
    )(proj, proj, proj, conv_w, small, alog, dtb)


def _gdn_kernel(q_ref, k_ref, v_ref, beta_ref, gc_ref, gt_ref, z_ref, gn_ref, o_ref, s_ref):
    c = GDN_CHUNK
    heads = range(DN_HEADS)

    @pl.when(pl.program_id(0) == 0)
    def _():
        s_ref[...] = jnp.zeros(s_ref.shape, F32)

    ii = lax.broadcasted_iota(jnp.int32, (c, c), 0)
    jj = lax.broadcasted_iota(jnp.int32, (c, c), 1)
    ge = ii >= jj
    gt = ii > jj
    eye = (ii == jj).astype(F32)
    blk_id = lambda t: (jnp.right_shift(ii, t), jnp.right_shift(jj, t))
    same = lambda t: blk_id(t)[0] == blk_id(t)[1]
    pair_mask = same(1)
    merge_masks = [same(t + 1) & jnp.logical_not(same(t)) for t in range(1, int(math.log2(c)))]

    def mm(a, b):
        return jnp.dot(a.astype(BF16), b.astype(BF16), preferred_element_type=F32)

    units = [(ci, hh) for ci in range(GDN_CHUNKS_PER_STEP) for hh in heads]
    rows = lambda u: slice(u[0] * c, (u[0] + 1) * c)
    cols = lambda u: slice(u[1] * LANES, (u[1] + 1) * LANES)
    q16 = {u: q_ref[rows(u), cols(u)] for u in units}
    k16 = {u: k_ref[rows(u), cols(u)] for u in units}
    k = {u: k16[u].astype(F32) for u in units}
    beta = {u: beta_ref[rows(u), u[1]:u[1] + 1] for u in units}
    g_col = {u: gc_ref[rows(u), u[1]:u[1] + 1] for u in units}
    g_last = {u: gc_ref[(u[0] + 1) * c - 1:(u[0] + 1) * c, u[1]:u[1] + 1] for u in units}
    decay = {u: jnp.exp(jnp.where(ge, g_col[u] - gt_ref[u[1]:u[1] + 1, rows(u)], -jnp.inf))
             for u in units}
    e_g = {u: jnp.exp(g_col[u]) for u in units}
    kb = {u: k[u] * beta[u] for u in units}
    vb = {u: v_ref[rows(u), cols(u)].astype(F32) * beta[u] for u in units}
    kq = {u: lax.dot_general(jnp.concatenate([kb[u].astype(BF16), q16[u]], axis=0), k16[u],
                             NT_DIMS, preferred_element_type=F32) for u in units}
    a = {u: jnp.where(gt, kq[u][:c] * decay[u], 0.0) for u in units}
    qk = {u: jnp.where(ge, kq[u][c:] * decay[u], 0.0) for u in units}
    x = {u: eye - jnp.where(pair_mask, a[u], 0.0) for u in units}
    for mask in merge_masks:
        y = {u: mm(jnp.where(mask, a[u], 0.0), x[u]) for u in units}
        x = {u: x[u] - mm(x[u], y[u]) for u in units}
    uw = {u: mm(x[u], jnp.concatenate([vb[u], kb[u] * e_g[u]], axis=1)) for u in units}
    wq = {u: jnp.concatenate([uw[u][:, DN_DV:], q16[u].astype(F32) * e_g[u]], axis=0) for u in units}
    kd = {u: (k[u] * jnp.exp(g_last[u] - g_col[u])).astype(BF16) for u in units}
    state = [s_ref[hh] for hh in heads]
    for ci in range(GDN_CHUNKS_PER_STEP):
        us = [(ci, hh) for hh in heads]
        ws = [mm(wq[u], state[u[1]]) for u in us]
        v_new = [uw[u][:, :DN_DV] - ws[u[1]][:c] for u in us]
        o = [ws[u[1]][c:] + mm(qk[u], v_new[u[1]]) for u in us]
        state = [state[u[1]] * jnp.exp(g_last[u]) + lax.dot_general(
            kd[u], v_new[u[1]].astype(BF16), TN_DIMS, preferred_element_type=F32) for u in us]
        for u in us:
            ou = o[u[1]]
            ms = jnp.mean(ou * ou, axis=-1, keepdims=True)
            y = (((ou * lax.rsqrt(ms + RMS_EPS)) * gn_ref[...])
                 * _silu(z_ref[rows(u), cols(u)].astype(F32)))
            o_ref[rows(u), cols(u)] = y.astype(BF16)
    for hh in heads:
        s_ref[hh] = state[hh]


def _gdn(qn, kn, vn, beta, gcum, gcum_t, proj, gn):
    lp = qn.shape[0]
    r = GDN_CHUNK * GDN_CHUNKS_PER_STEP
    blk = lambda j: pl.BlockSpec((r, BLK), lambda n, j=j: (n, j))
    row = pl.BlockSpec((r, LANES), lambda n: (n, 0))
    return pl.pallas_call(
        _gdn_kernel,
        out_shape=jax.ShapeDtypeStruct((lp, DN_HEADS * DN_DV), BF16),
        grid=(lp // r,),
        in_specs=[blk(0), blk(0), blk(0), row, row,
                  pl.BlockSpec((SUBLANES, r), lambda n: (0, n)),
                  blk(6), pl.BlockSpec((1, DN_DV), lambda n: (0, 0))],
        out_specs=blk(0),
        scratch_shapes=[pltpu.VMEM((DN_HEADS, DN_DK, DN_DV), F32)],
        compiler_params=_params(("arbitrary",)),
        name="gdn",
    )(qn, kn, vn, beta, gcum, gcum_t, proj, gn)


def _mix_kernel(ao_ref, do_ref, ga_ref, gb_ref, h_ref, wa_ref, wb_ref, wo_ref, o_ref):
    ya = jnp.dot(ao_ref[...], wa_ref[...], preferred_element_type=F32)
    yb = jnp.dot(do_ref[...], wb_ref[...], preferred_element_type=F32)
    mixed = (jax.nn.sigmoid(ga_ref[...].astype(F32)) * ya
             + jax.nn.sigmoid(gb_ref[...].astype(F32)) * yb)
    o_ref[...] = h_ref[...] + jnp.dot(mixed.astype(BF16), wo_ref[...],
                                      preferred_element_type=F32)


def _mix(ao, do, proj, h, wa, wb, wo, tm):
    lp = h.shape[0]
    blk = lambda j: pl.BlockSpec((tm, BLK), lambda m, j=j: (m, j))
    wspec = pl.BlockSpec((BLK, D_MODEL), lambda m: (0, 0))
    return pl.pallas_call(
        _mix_kernel,
        out_shape=jax.ShapeDtypeStruct((lp, D_MODEL), F32),
        grid=(lp // tm,),
        in_specs=[blk(0), blk(0), blk(7), blk(8), blk(0), wspec, wspec, wspec],
        out_specs=blk(0),
        compiler_params=_params(("parallel",)),
        name="mix",
    )(ao, do, proj, proj, h, wa, wb, wo)


def _ffn_kernel(h_ref, g_ref, wup_ref, cw_ref, wdn_ref, o_ref, fbuf_ref, *, tc):
    tm = h_ref.shape[0]
    halo = SUBLANES

    @pl.when(pl.program_id(0) == 0)
    def _():
        fbuf_ref[0:halo, :] = jnp.zeros((halo, 2 * D_FF), F32)

    @pl.when(pl.program_id(0) > 0)
    def _():
        fbuf_ref[0:halo, :] = fbuf_ref[tm:tm + halo, :]

    x = h_ref[...]
    ms = jnp.mean(x * x, axis=-1, keepdims=True)
    u = ((x * lax.rsqrt(ms + RMS_EPS)) * g_ref[...]).astype(BF16)
    for j in range(2 * D_FF // tc):
        sl = slice(j * tc, (j + 1) * tc)
        fbuf_ref[halo:, sl] = jnp.dot(u, wup_ref[:, sl], preferred_element_type=F32)

    def conv(sl):
        acc = None
        for t in range(FFN_CONV):
            off = halo - (FFN_CONV - 1) + t
            term = fbuf_ref[off:off + tm, sl] * cw_ref[t:t + 1, sl]
            acc = term if acc is None else acc + term
        return acc

    acc = x
    for j in range(D_FF // tc):
        gate = conv(slice(j * tc, (j + 1) * tc))
        up = conv(slice(D_FF + j * tc, D_FF + (j + 1) * tc))
        act = (_silu(gate) * up).astype(BF16)
        acc = acc + jnp.dot(act, wdn_ref[j * tc:(j + 1) * tc, :], preferred_element_type=F32)
    o_ref[...] = acc


def _ffn(h, g, wup, cw, wdn, tm, tc=256):
    lp = h.shape[0]
    once = pl.Buffered(1)
    return pl.pallas_call(
        functools.partial(_ffn_kernel, tc=tc),
        out_shape=jax.ShapeDtypeStruct((lp, D_MODEL), F32),
        grid=(lp // tm,),
        in_specs=[pl.BlockSpec((tm, D_MODEL), lambda m: (m, 0)),
                  pl.BlockSpec((1, D_MODEL), lambda m: (0, 0)),
                  pl.BlockSpec((D_MODEL, 2 * D_FF), lambda m: (0, 0), pipeline_mode=once),
                  pl.BlockSpec((FFN_CONV, 2 * D_FF), lambda m: (0, 0)),
                  pl.BlockSpec((D_FF, D_MODEL), lambda m: (0, 0), pipeline_mode=once)],
        out_specs=pl.BlockSpec((tm, D_MODEL), lambda m: (m, 0)),
        scratch_shapes=[pltpu.VMEM((tm + SUBLANES, 2 * D_FF), F32)],
        compiler_params=_params(("arbitrary",)),
        name="ffn",
    )(h, g, wup, cw, wdn)


def _regroup_w_in(w):
    a_cols = 3 * A_HEADS * A_DV
    dn_cols = DN_HEADS * (2 * DN_DK + DN_DV) + DN_HEADS * DN_DV
    o_small = a_cols + dn_cols
    w_main = jnp.concatenate([w[:, :o_small], w[:, o_small + 2 * DN_HEADS:]], axis=1)
    w_small = jnp.zeros((D_MODEL, 2 * LANES), w.dtype)
    w_small = w_small.at[:, 0:DN_HEADS].set(w[:, o_small:o_small + DN_HEADS])
    w_small = w_small.at[:, LANES:LANES + DN_HEADS].set(
        w[:, o_small + DN_HEADS:o_small + 2 * DN_HEADS])
    return w_main.astype(BF16), w_small.astype(BF16)


def _pad_lanes(v):
    return jnp.zeros((1, LANES), F32).at[0, :v.shape[0]].set(v.astype(F32))


def kernel(x, meta_tokens, mix_norm_g, w_in, q_norm_g, k_norm_g, lambda_q1, lambda_k1, lambda_q2, lambda_k2, attn_subln_g, dn_conv_w, dn_a_log, dn_dt_bias, dn_norm_g, w_branch_attn, w_branch_dn, w_out, ffn_norm_g, w_ffn_up, ffn_conv_w, w_ffn_down):
    batch, seq, _ = x.shape
    assert batch == 1
    depth = w_in.shape[0]
    length = N_META + seq
    lp = -(-length // ROW_ALIGN) * ROW_ALIGN
    tm = _pick_tile(lp, (768, 512))
    tm_ffn = _pick_tile(lp, (384, 256))
    qtab, ktab = _alibi_tables()
    h = jnp.concatenate([meta_tokens.astype(F32), x[0],
                         jnp.zeros((lp - length, D_MODEL), F32)], axis=0)
    row = lambda v: v.astype(F32)[None, :]
    for layer in range(depth):
        lam_init = 0.8 - 0.6 * math.exp(-0.3 * layer)
        w_main, w_small = _regroup_w_in(w_in[layer])
        proj, small = _inproj(h, row(mix_norm_g[layer]), w_main, w_small, tm)

        gq2 = jnp.tile(row(q_norm_g[layer]), (1, 2))
        gk2 = jnp.tile(row(k_norm_g[layer]), (1, 2))
        qn, ka, vb = _qkprep(proj, gq2, gk2, ktab, tm)
        bound = (A_DH ** 0.5) * LOG2E * jnp.max(jnp.abs(gq2)) * jnp.max(jnp.abs(gk2))
        ao = _attn(qn, ka, vb, bound, qtab, row(lambda_q1[layer]), row(lambda_k1[layer]),
                   row(lambda_q2[layer]), row(lambda_k2[layer]), row(attn_subln_g[layer]),
                   lam_init)

        dq, dk, dv, beta, gcum, gcum_t = _gdnprep(
            proj, small, dn_conv_w[layer].astype(F32), _pad_lanes(dn_a_log[layer]),
            _pad_lanes(dn_dt_bias[layer]), tm)
        do = _gdn(dq, dk, dv, beta, gcum, gcum_t, proj, row(dn_norm_g[layer]))

        h = _mix(ao, do, proj, h, w_branch_attn[layer].astype(BF16),
                 w_branch_dn[layer].astype(BF16), w_out[layer].astype(BF16), tm)
        h = _ffn(h, row(ffn_norm_g[layer]), w_ffn_up[layer].astype(BF16),
                 ffn_conv_w[layer].astype(F32), w_ffn_down[layer].astype(BF16), tm_ffn)
    return h[N_META:N_META + seq][None]
```

```python
import functools
import math

import jax
import jax.numpy as jnp
import ml_dtypes
import numpy as np
from jax import lax
from jax.experimental import pallas as pl
from jax.experimental.pallas import tpu as pltpu

D_MODEL = 1024
N_META = 16
A_HEADS = 8
A_DH = 64
A_DV = 2 * A_DH
DN_HEADS = 8
DN_DK = 128
DN_DV = 128
DN_CONV = 4
D_FF = 2816
FFN_CONV = 3
RMS_EPS = 1e-6
L2_EPS = 1e-6
ALIBI_MAX_EXP = 8.0

LANES = 128
SUBLANES = 8
ROW_ALIGN = 512
ATTN_TQ = 512
ATTN_TK = 256
ATTN_GROUP = 8
ATTN_HALF_GROUP = 4
SAFE_TQ = 256
SAFE_TK_BIG = 1024
GDN_CHUNK = 128
GDN_CHUNKS_PER_STEP = 2
BLK = 1024
LOG2E = 1.4426950408889634
VMEM_LIMIT = 52 * 1024 * 1024

FAST_EXP_BOUND = 60.0
ZERO_SCALE_EXP = 150.0

AUG_ROWB = 0
AUG_ROWA = 3
AUG_JR = 6
AUG_ONE = 9
AUG_JREL = 12
N_PIECES = 3
assert ATTN_TQ // ATTN_TK <= 2 and ATTN_TK == 256
AUG_BASE = (A_DH, 0)
N_QVAR = 1 + ATTN_TQ // ATTN_TK
QTAB_SLOPE_ROW = 2 * N_QVAR
QTAB_ROWS = 8

F32 = jnp.float32
BF16 = jnp.bfloat16
NT_DIMS = (((1,), (1,)), ((), ()))
TN_DIMS = (((0,), (0,)), ((), ()))


def _params(sem, vmem=VMEM_LIMIT):
    return pltpu.CompilerParams(dimension_semantics=sem, vmem_limit_bytes=vmem)


def _pick_tile(n, candidates):
    for c in candidates:
        if n % c == 0:
            return c
    raise ValueError(f"no tile in {candidates} divides {n}")


def _silu(x):
    return x * jax.nn.sigmoid(x)


def _bf16_pieces(x):
    out, r = [], np.float64(x)
    for _ in range(N_PIECES):
        p = float(np.float32(r).astype(ml_dtypes.bfloat16).astype(np.float32))
        out.append(p)
        r -= p
    return out


def _alibi_tables():
    qtab = np.zeros((A_HEADS, QTAB_ROWS, LANES), np.float32)
    ktab = np.zeros((2 * A_HEADS, LANES), np.float32)
    for h in range(A_HEADS):
        slope = 2.0 ** (-(ALIBI_MAX_EXP / A_HEADS) * (h + 1))
        pieces = _bf16_pieces(slope * LOG2E)
        qtab[h, QTAB_SLOPE_ROW, :] = np.float32(sum(pieces))
        for c in range(2):
            b = AUG_BASE[c]
            for i, p in enumerate(pieces):
                ktab[2 * h + c, b + AUG_ROWB + i] = -p
                ktab[2 * h + c, b + AUG_ROWA + i] = -float(ATTN_TK) * p
                ktab[2 * h + c, b + AUG_ONE + i] = 1.0
                qtab[h, N_QVAR * c, b + AUG_JR + i] = -p
                for d in range(ATTN_TQ // ATTN_TK):
                    qtab[h, N_QVAR * c + 1 + d, b + AUG_JREL + i] = p
                    qtab[h, N_QVAR * c + 1 + d, b + AUG_ONE + i] = p * ATTN_TK * d
    return jnp.asarray(qtab), jnp.asarray(ktab)


def _inproj_kernel(h_ref, g_ref, w_ref, ws_ref, o_ref, os_ref, u_ref):
    @pl.when(pl.program_id(1) == 0)
    def _():
        x = h_ref[...]
        ms = jnp.mean(x * x, axis=-1, keepdims=True)
        u = ((x * lax.rsqrt(ms + RMS_EPS)) * g_ref[...]).astype(BF16)
        u_ref[...] = u
        os_ref[...] = jnp.dot(u, ws_ref[...], preferred_element_type=F32)

    o_ref[...] = jnp.dot(u_ref[...], w_ref[...], preferred_element_type=F32).astype(o_ref.dtype)


def _inproj(h, g, w_main, w_small, tm, tn=BLK):
    lp = h.shape[0]
    n_main = w_main.shape[1]
    n_small = w_small.shape[1]
    return pl.pallas_call(
        _inproj_kernel,
        out_shape=(jax.ShapeDtypeStruct((lp, n_main), BF16),
                   jax.ShapeDtypeStruct((lp, n_small), F32)),
        grid=(lp // tm, n_main // tn),
        in_specs=[
            pl.BlockSpec((tm, D_MODEL), lambda m, n: (m, 0)),
            pl.BlockSpec((1, D_MODEL), lambda m, n: (0, 0)),
            pl.BlockSpec((D_MODEL, tn), lambda m, n: (0, n)),
            pl.BlockSpec((D_MODEL, n_small), lambda m, n: (0, 0)),
        ],
        out_specs=(
            pl.BlockSpec((tm, tn), lambda m, n: (m, n)),
            pl.BlockSpec((tm, n_small), lambda m, n: (m, 0)),
        ),
        scratch_shapes=[pltpu.VMEM((tm, D_MODEL), BF16)],
        compiler_params=_params(("parallel", "arbitrary")),
        name="inproj",
    )(h, g, w_main, w_small)


def _qkprep_kernel(aq_ref, ak_ref, av_ref, gq_ref, gk_ref, ktab_ref, q_out, k_out, v_out):
    tm = aq_ref.shape[0]
    lane = lax.broadcasted_iota(jnp.int32, (tm, LANES), 1)
    lo = lane < A_DH
    q_scale = (A_DH ** -0.5) * LOG2E

    def norm_pair(x, g):
        ss = x * x
        s_lo = jnp.sum(jnp.where(lo, ss, 0.0), axis=-1, keepdims=True)
        s_hi = jnp.sum(jnp.where(lo, 0.0, ss), axis=-1, keepdims=True)
        ms = jnp.where(lo, s_lo, s_hi) * (1.0 / A_DH)
        return (x * lax.rsqrt(ms + RMS_EPS)) * g

    j = (pl.program_id(0) * tm + lax.broadcasted_iota(jnp.int32, (tm, LANES), 0))
    jrel = jnp.bitwise_and(j, ATTN_TK - 1)
    jr = (ATTN_TK - 1) - jrel
    pos = []
    for c in range(2):
        la = lane - AUG_BASE[c]
        pick = lambda off, la=la: (la >= off) & (la < off + N_PIECES)
        pos.append(jnp.where(pick(AUG_JR), jr, jnp.where(pick(AUG_JREL), jrel, 0)).astype(F32))

    for hh in range(A_HEADS):
        sl = slice(hh * LANES, (hh + 1) * LANES)
        yq = norm_pair(aq_ref[:, sl].astype(F32), gq_ref[...]) * q_scale
        yk = norm_pair(ak_ref[:, sl].astype(F32), gk_ref[...])
        q_out[:, sl] = yq.astype(BF16)
        k_out[:, 2 * hh * LANES:(2 * hh + 1) * LANES] = jnp.where(
            lo, yk, pos[0] + ktab_ref[2 * hh:2 * hh + 1, :]).astype(BF16)
        k_out[:, (2 * hh + 1) * LANES:(2 * hh + 2) * LANES] = jnp.where(
            lo, pos[1] + ktab_ref[2 * hh + 1:2 * hh + 2, :], yk).astype(BF16)
    v_out[...] = av_ref[...]


def _qkprep(proj, gq2, gk2, ktab, tm):
    lp = proj.shape[0]
    blk = lambda j: pl.BlockSpec((tm, BLK), lambda m, j=j: (m, j))
    vec = pl.BlockSpec((1, LANES), lambda m: (0, 0))
    out = jax.ShapeDtypeStruct((lp, BLK), BF16)
    return pl.pallas_call(
        _qkprep_kernel,
        out_shape=(out, jax.ShapeDtypeStruct((lp, 2 * BLK), BF16), out),
        grid=(lp // tm,),
        in_specs=[blk(0), blk(1), blk(2), vec, vec,
                  pl.BlockSpec((2 * A_HEADS, LANES), lambda m: (0, 0))],
        out_specs=(blk(0), pl.BlockSpec((tm, 2 * BLK), lambda m: (m, 0)), blk(0)),
        compiler_params=_params(("parallel",)),
        name="qkprep",
    )(proj, proj, proj, gq2, gk2, ktab)


def _attn_finish(lq1_ref, lk1_ref, lq2_ref, lk2_ref, g_ref, o_ref, o1, o2, lam_init):
    lam = (jnp.exp(jnp.sum(lq1_ref[...] * lk1_ref[...], axis=-1, keepdims=True))
           - jnp.exp(jnp.sum(lq2_ref[...] * lk2_ref[...], axis=-1, keepdims=True))
           + lam_init)
    o = o1 - lam * o2
    ms = jnp.mean(o * o, axis=-1, keepdims=True)
    y = ((o * lax.rsqrt(ms + RMS_EPS)) * g_ref[...]) * (1.0 - lam_init)
    o_ref[...] = y.astype(BF16)


def _attn_fast_kernel(lq1_ref, lk1_ref, lq2_ref, lk2_ref, qtab_ref, q_ref, k_ref, v_ref, g_ref,
                      o_ref, qa_ref, l_ref, acc_ref, *, lam_init):
    tq, tk, grp = ATTN_TQ, ATTN_TK, ATTN_GROUP
    n_diag = tq // tk
    log_tk = int(math.log2(tk))
    h = pl.program_id(0)
    i = pl.program_id(1)
    row0 = i * tq
    tab = qtab_ref[0]
    slope2 = tab[QTAB_SLOPE_ROW:QTAB_SLOPE_ROW + 1, :]
    q = q_ref[...].astype(F32)
    lane = lax.broadcasted_iota(jnp.int32, (tq, LANES), 1)
    rowi = lax.broadcasted_iota(jnp.int32, (tq, LANES), 0)
    row_b = jnp.bitwise_and(rowi, tk - 1).astype(F32)
    row_a = jnp.right_shift(rowi, log_tk).astype(F32)
    tab16 = tab.astype(BF16)
    for c in range(2):
        content = (lane < A_DH) if c == 0 else (lane >= A_DH)
        la = lane - AUG_BASE[c]
        pick = lambda off, la=la: (la >= off) & (la < off + N_PIECES)
        rows = jnp.where(pick(AUG_ROWB), row_b, jnp.where(pick(AUG_ROWA), row_a, 0.0))
        base = jnp.where(content, q, rows).astype(BF16)
        for variant in range(N_QVAR):
            qa_ref[N_QVAR * c + variant] = base + tab16[N_QVAR * c + variant:N_QVAR * c + variant + 1, :]

    def scores(c, variant, start, rows=slice(None)):
        s = lax.dot_general(qa_ref[N_QVAR * c + variant, rows, :],
                            k_ref[pl.ds(start, tk), c * LANES:(c + 1) * LANES],
                            NT_DIMS, preferred_element_type=F32)
        return s

    slope2_tile = jnp.concatenate([slope2] * (tk // LANES), axis=1)

    def far_tiles(start, n_tiles, scale_dist):
        scale = jnp.exp2(-slope2 * scale_dist.astype(F32))
        for c in range(2):
            pv = None
            lsum = None
            for t in range(n_tiles):
                s = scores(c, 0, start + t * tk)
                after = n_tiles - 1 - t
                if after:
                    s = s - slope2_tile * float(tk * after)
                p = jnp.exp2(s)
                part = p[:, :LANES] + p[:, LANES:]
                lsum = part if lsum is None else lsum + part
                d = jnp.dot(p.astype(BF16), v_ref[pl.ds(start + t * tk, tk), :],
                            preferred_element_type=F32)
                pv = d if pv is None else pv + d
            acc_ref[c] += pv * scale
            l_ref[c] += lsum * scale

    l_ref[...] = jnp.zeros(l_ref.shape, F32)
    acc_ref[...] = jnp.zeros(acc_ref.shape, F32)

    assert ALIBI_MAX_EXP == A_HEADS
    zero_dist = jnp.left_shift(jnp.int32(math.ceil(ZERO_SCALE_EXP / LOG2E * 2.0)), h)
    avail = jnp.right_shift(row0, log_tk)
    first_tile = jnp.right_shift(jnp.maximum(row0 - zero_dist, 0), log_tk)
    need = avail - first_tile
    half = ATTN_HALF_GROUP
    n_groups = jnp.minimum((need + grp - half - 1) // grp, avail // grp)
    group_tile0 = avail - n_groups * grp
    left = jnp.maximum(group_tile0 - first_tile, 0)
    n_half = jnp.where((left > 0) & (group_tile0 >= half), 1, 0)
    half_tile0 = group_tile0 - n_half * half
    n_single = jnp.maximum(half_tile0 - first_tile, 0)

    def single_body(js, carry):
        start = pl.multiple_of((first_tile + js) * tk, tk)
        far_tiles(start, 1, row0 - (start + tk - 1))
        return carry

    lax.fori_loop(0, n_single, single_body, 0)

    n_odd = jnp.bitwise_and(n_groups, 1)

    def group_body(jg, carry):
        start = pl.multiple_of(group_tile0 * tk, tk)
        far_tiles(start, grp, row0 - (start + grp * tk - 1))
        return carry

    lax.fori_loop(0, n_odd, group_body, 0)

    def pair_body(jp, carry):
        start = pl.multiple_of((group_tile0 + (n_odd + 2 * jp) * grp) * tk, tk)
        far_tiles(start, 2 * grp, row0 - (start + 2 * grp * tk - 1))
        return carry

    lax.fori_loop(0, jnp.right_shift(n_groups, 1), pair_body, 0)

    def diag_tiles():
        for d in range(n_diag):
            rows = slice(d * tk, tq)
            n_rows = tq - d * tk
            keep = (lax.broadcasted_iota(jnp.int32, (1, tk), 1)
                    <= lax.broadcasted_iota(jnp.int32, (n_rows, 1), 0))
            start = pl.multiple_of(row0 + d * tk, tk)
            for c in range(2):
                s = jnp.where(keep, scores(c, 1 + d, start, rows), -jnp.inf)
                p = jnp.exp2(s)
                acc_ref[c, rows, :] += jnp.dot(p.astype(BF16), v_ref[pl.ds(start, tk), :],
                                               preferred_element_type=F32)
                l_ref[c, rows, :] += p[:, :LANES] + p[:, LANES:]

    def finish():
        o1 = acc_ref[0] / jnp.sum(l_ref[0], axis=-1, keepdims=True)
        o2 = acc_ref[1] / jnp.sum(l_ref[1], axis=-1, keepdims=True)
        _attn_finish(lq1_ref, lk1_ref, lq2_ref, lk2_ref, g_ref, o_ref, o1, o2, lam_init)

    @pl.when(n_half == 1)
    def _():
        start = pl.multiple_of(half_tile0 * tk, tk)
        far_tiles(start, half, row0 - (start + half * tk - 1))
        diag_tiles()
        finish()

    @pl.when(n_half == 0)
    def _():
        diag_tiles()
        finish()


def _attn_safe_kernel(lq1_ref, lk1_ref, lq2_ref, lk2_ref, q_ref, k_ref, v_ref, g_ref, o_ref,
                      m_ref, l_ref, acc_ref, *, lam_init):
    tq, tb = SAFE_TQ, SAFE_TK_BIG
    h = pl.program_id(0)
    i = pl.program_id(1)
    slope2 = jnp.exp2(-(ALIBI_MAX_EXP / A_HEADS) * (h + 1).astype(F32)) * LOG2E
    q = q_ref[...].astype(F32)
    lane = lax.broadcasted_iota(jnp.int32, (tq, LANES), 1)
    q_maps = (jnp.where(lane < A_DH, q, 0.0).astype(BF16), jnp.where(lane >= A_DH, q, 0.0).astype(BF16))
    m_ref[...] = jnp.full(m_ref.shape, -jnp.inf, F32)
    l_ref[...] = jnp.zeros(l_ref.shape, F32)
    acc_ref[...] = jnp.zeros(acc_ref.shape, F32)
    row0 = i * tq

    def block(start, size, masked):
        v = v_ref[pl.ds(start, size), :]
        col = lax.broadcasted_iota(jnp.int32, (1, size), 1) + (start - row0)
        bias = slope2 * col.astype(F32)
        if masked:
            keep = col <= lax.broadcasted_iota(jnp.int32, (tq, 1), 0)
        for c in range(2):
            k = k_ref[pl.ds(start, size), c * LANES:(c + 1) * LANES]
            s = lax.dot_general(q_maps[c], k, NT_DIMS, preferred_element_type=F32) + bias
            if masked:
                s = jnp.where(keep, s, -jnp.inf)
            m_prev = m_ref[c]
            m_new = jnp.maximum(m_prev, jnp.max(s, axis=-1, keepdims=True))
            alpha = jnp.exp2(m_prev - m_new)
            p = jnp.exp2(s - m_new)
            l_ref[c] = alpha * l_ref[c] + jnp.sum(p, axis=-1, keepdims=True)
            acc_ref[c] = alpha * acc_ref[c] + jnp.dot(p.astype(BF16), v,
                                                      preferred_element_type=F32)
            m_ref[c] = m_new

    n_big = row0 // tb

    def big_body(jb, carry):
        block(pl.multiple_of(jb * tb, tb), tb, False)
        return carry

    lax.fori_loop(0, n_big, big_body, 0)
    base = n_big * tb
    n_small = (row0 - base) // tq

    def small_body(js, carry):
        block(pl.multiple_of(base + js * tq, tq), tq, False)
        return carry

    lax.fori_loop(0, n_small, small_body, 0)
    block(pl.multiple_of(row0, tq), tq, True)
    _attn_finish(lq1_ref, lk1_ref, lq2_ref, lk2_ref, g_ref, o_ref,
                 acc_ref[0] / l_ref[0], acc_ref[1] / l_ref[1], lam_init)


def _attn(qn, ka, vb, bound, qtab, lq1, lk1, lq2, lk2, g_sub, lam_init):
    lp = qn.shape[0]
    lamspec = pl.BlockSpec((1, A_DH), lambda h, i: (0, 0))
    lams = [lamspec] * 4

    def common(tq):
        return dict(
            out_shape=jax.ShapeDtypeStruct((lp, A_HEADS * A_DV), BF16),
            grid=(A_HEADS, lp // tq),
            out_specs=pl.BlockSpec((tq, A_DV), lambda h, i: (i, h)),
            compiler_params=_params(("parallel", "arbitrary")),
        )

    def data_specs(tq):
        return [pl.BlockSpec((tq, LANES), lambda h, i: (i, h)),
                pl.BlockSpec((lp, 2 * LANES), lambda h, i: (0, h)),
                pl.BlockSpec((lp, LANES), lambda h, i: (0, h)),
                pl.BlockSpec((1, A_DV), lambda h, i: (0, 0))]

    def fast(qn, ka, vb):
        tq = ATTN_TQ
        return pl.pallas_call(
            functools.partial(_attn_fast_kernel, lam_init=lam_init),
            in_specs=(lams + [pl.BlockSpec((1, QTAB_ROWS, LANES), lambda h, i: (h, 0, 0))]
                      + data_specs(tq)),
            scratch_shapes=[pltpu.VMEM((2 * N_QVAR, tq, LANES), BF16),
                            pltpu.VMEM((2, tq, LANES), F32), pltpu.VMEM((2, tq, A_DV), F32)],
            name="attn_fast", **common(tq),
        )(lq1, lk1, lq2, lk2, qtab, qn, ka, vb, g_sub)

    def safe(qn, ka, vb):
        tq = SAFE_TQ
        return pl.pallas_call(
            functools.partial(_attn_safe_kernel, lam_init=lam_init),
            in_specs=lams + data_specs(tq),
            scratch_shapes=[pltpu.VMEM((2, tq, 1), F32), pltpu.VMEM((2, tq, 1), F32),
                            pltpu.VMEM((2, tq, A_DV), F32)],
            name="attn_safe", **common(tq),
        )(lq1, lk1, lq2, lk2, qn, ka, vb, g_sub)

    return lax.cond(bound <= FAST_EXP_BOUND, fast, safe, qn, ka, vb)


def _gdnprep_kernel(xq_ref, xk_ref, xv_ref, cw_ref, sm_ref, alog_ref, dtb_ref,
                    q_out, k_out, v_out, beta_out, gcum_out, gcumt_out, xbuf_ref):
    tm = xq_ref.shape[0]
    halo = SUBLANES

    @pl.when(pl.program_id(0) == 0)
    def _():
        xbuf_ref[0:halo, :] = jnp.zeros((halo, 3 * BLK), F32)

    xbuf_ref[halo:, 0:BLK] = xq_ref[...].astype(F32)
    xbuf_ref[halo:, BLK:2 * BLK] = xk_ref[...].astype(F32)
    xbuf_ref[halo:, 2 * BLK:] = xv_ref[...].astype(F32)

    outs = (q_out, k_out, v_out)
    for s in range(3 * BLK // LANES):
        sl = slice(s * LANES, (s + 1) * LANES)
        xw = xbuf_ref[:, sl]
        acc = xw[halo:] * cw_ref[DN_CONV - 1:DN_CONV, sl]
        for back in range(1, DN_CONV):
            shifted = pltpu.roll(xw, back, axis=0)[halo:]
            acc = acc + shifted * cw_ref[DN_CONV - 1 - back:DN_CONV - back, sl]
        y = _silu(acc)
        part, hh = divmod(s, DN_HEADS)
        if part < 2:
            y = y * lax.rsqrt(jnp.sum(y * y, axis=-1, keepdims=True) + L2_EPS)
            if part == 0:
                y = y * (DN_DK ** -0.5)
        outs[part][:, hh * LANES:(hh + 1) * LANES] = y.astype(BF16)

    xbuf_ref[0:halo, :] = xbuf_ref[tm:tm + halo, :]

    beta_out[...] = jax.nn.sigmoid(sm_ref[:, 0:LANES])
    g = -jnp.exp(alog_ref[...]) * jax.nn.softplus(sm_ref[:, LANES:] + dtb_ref[...])
    g1 = g.astype(BF16)
    r1 = g - g1.astype(F32)
    g2 = r1.astype(BF16)
    g3 = (r1 - g2.astype(F32)).astype(BF16)
    ii = lax.broadcasted_iota(jnp.int32, (GDN_CHUNK, GDN_CHUNK), 0)
    jj = lax.broadcasted_iota(jnp.int32, (GDN_CHUNK, GDN_CHUNK), 1)
    tril = (ii >= jj).astype(BF16)
    for cidx in range(tm // GDN_CHUNK):
        rs = slice(cidx * GDN_CHUNK, (cidx + 1) * GDN_CHUNK)
        gc = (jnp.dot(tril, g1[rs], preferred_element_type=F32)
              + jnp.dot(tril, g2[rs], preferred_element_type=F32)
              + jnp.dot(tril, g3[rs], preferred_element_type=F32))
        gcum_out[rs, :] = gc
        gcumt_out[:, rs] = gc.T[0:SUBLANES, :]


def _gdnprep(proj, small, conv_w, alog, dtb, tm):
    lp = proj.shape[0]
    blk = lambda j: pl.BlockSpec((tm, BLK), lambda m, j=j: (m, j))
    row = pl.BlockSpec((tm, LANES), lambda m: (m, 0))
    vec = pl.BlockSpec((1, LANES), lambda m: (0, 0))
    o16 = jax.ShapeDtypeStruct((lp, BLK), BF16)
    o32 = jax.ShapeDtypeStruct((lp, LANES), F32)
    return pl.pallas_call(
        _gdnprep_kernel,
        out_shape=(o16, o16, o16, o32, o32, jax.ShapeDtypeStruct((SUBLANES, lp), F32)),
        grid=(lp // tm,),
        in_specs=[blk(3), blk(4), blk(5),
                  pl.BlockSpec((DN_CONV, 3 * BLK), lambda m: (0, 0)),
                  pl.BlockSpec((tm, 2 * LANES), lambda m: (m, 0)), vec, vec],
        out_specs=(blk(0), blk(0), blk(0), row, row,
                   pl.BlockSpec((SUBLANES, tm), lambda m: (0, m))),
        scratch_shapes=[pltpu.VMEM((tm + SUBLANES, 3 * BLK), F32)],
        compiler_params=_params(("arbitrary",)),
        name="gdnprep",
    )(proj, proj, proj, conv_w, small, alog, dtb)


def _gdn_kernel(q_ref, k_ref, v_ref, beta_ref, gc_ref, gt_ref, z_ref, gn_ref, o_ref, s_ref):
    c = GDN_CHUNK
    heads = range(DN_HEADS)

    @pl.when(pl.program_id(0) == 0)
    def _():
        s_ref[...] = jnp.zeros(s_ref.shape, F32)

    ii = lax.broadcasted_iota(jnp.int32, (c, c), 0)
    jj = lax.broadcasted_iota(jnp.int32, (c, c), 1)
    ge = ii >= jj
    gt = ii > jj
    eye = (ii == jj).astype(F32)
    blk_id = lambda t: (jnp.right_shift(ii, t), jnp.right_shift(jj, t))
    same = lambda t: blk_id(t)[0] == blk_id(t)[1]
    pair_mask = same(1)
    merge_masks = [same(t + 1) & jnp.logical_not(same(t)) for t in range(1, int(math.log2(c)))]

    def mm(a, b):
        return jnp.dot(a.astype(BF16), b.astype(BF16), preferred_element_type=F32)

    units = [(ci, hh) for ci in range(GDN_CHUNKS_PER_STEP) for hh in heads]
    rows = lambda u: slice(u[0] * c, (u[0] + 1) * c)
    cols = lambda u: slice(u[1] * LANES, (u[1] + 1) * LANES)
    q16 = {u: q_ref[rows(u), cols(u)] for u in units}
    k16 = {u: k_ref[rows(u), cols(u)] for u in units}
    k = {u: k16[u].astype(F32) for u in units}
    beta = {u: beta_ref[rows(u), u[1]:u[1] + 1] for u in units}
    g_col = {u: gc_ref[rows(u), u[1]:u[1] + 1] for u in units}
    g_last = {u: gc_ref[(u[0] + 1) * c - 1:(u[0] + 1) * c, u[1]:u[1] + 1] for u in units}
    decay = {u: jnp.exp(jnp.where(ge, g_col[u] - gt_ref[u[1]:u[1] + 1, rows(u)], -jnp.inf))
             for u in units}
    e_g = {u: jnp.exp(g_col[u]) for u in units}
    kb = {u: k[u] * beta[u] for u in units}
    vb = {u: v_ref[rows(u), cols(u)].astype(F32) * beta[u] for u in units}
    kq = {u: lax.dot_general(jnp.concatenate([kb[u].astype(BF16), q16[u]], axis=0), k16[u],
                             NT_DIMS, preferred_element_type=F32) for u in units}
    a = {u: jnp.where(gt, kq[u][:c] * decay[u], 0.0) for u in units}
    qk = {u: jnp.where(ge, kq[u][c:] * decay[u], 0.0) for u in units}
    x = {u: eye - jnp.where(pair_mask, a[u], 0.0) for u in units}
    for mask in merge_masks:
        y = {u: mm(jnp.where(mask, a[u], 0.0), x[u]) for u in units}
        x = {u: x[u] - mm(x[u], y[u]) for u in units}
    uw = {u: mm(x[u], jnp.concatenate([vb[u], kb[u] * e_g[u]], axis=1)) for u in units}
    wq = {u: jnp.concatenate([uw[u][:, DN_DV:], q16[u].astype(F32) * e_g[u]], axis=0) for u in units}
    kd = {u: (k[u] * jnp.exp(g_last[u] - g_col[u])).astype(BF16) for u in units}
    state = [s_ref[hh] for hh in heads]
    for ci in range(GDN_CHUNKS_PER_STEP):
        us = [(ci, hh) for hh in heads]
        ws = [mm(wq[u], state[u[1]]) for u in us]
        v_new = [uw[u][:, :DN_DV] - ws[u[1]][:c] for u in us]
        o = [ws[u[1]][c:] + mm(qk[u], v_new[u[1]]) for u in us]
        state = [state[u[1]] * jnp.exp(g_last[u]) + lax.dot_general(
            kd[u], v_new[u[1]].astype(BF16), TN_DIMS, preferred_element_type=F32) for u in us]
        for u in us:
            ou = o[u[1]]
            ms = jnp.mean(ou * ou, axis=-1, keepdims=True)
            y = (((ou * lax.rsqrt(ms + RMS_EPS)) * gn_ref[...])
                 * _silu(z_ref[rows(u), cols(u)].astype(F32)))
            o_ref[rows(u), cols(u)] = y.astype(BF16)
    for hh in heads:
        s_ref[hh] = state[hh]


def _gdn(qn, kn, vn, beta, gcum, gcum_t, proj, gn):
    lp = qn.shape[0]
    r = GDN_CHUNK * GDN_CHUNKS_PER_STEP
    blk = lambda j: pl.BlockSpec((r, BLK), lambda n, j=j: (n, j))
    row = pl.BlockSpec((r, LANES), lambda n: (n, 0))
    return pl.pallas_call(
        _gdn_kernel,
        out_shape=jax.ShapeDtypeStruct((lp, DN_HEADS * DN_DV), BF16),
        grid=(lp // r,),
        in_specs=[blk(0), blk(0), blk(0), row, row,
                  pl.BlockSpec((SUBLANES, r), lambda n: (0, n)),
                  blk(6), pl.BlockSpec((1, DN_DV), lambda n: (0, 0))],
        out_specs=blk(0),
        scratch_shapes=[pltpu.VMEM((DN_HEADS, DN_DK, DN_DV), F32)],
        compiler_params=_params(("arbitrary",)),
        name="gdn",
    )(qn, kn, vn, beta, gcum, gcum_t, proj, gn)


def _mix_kernel(ao_ref, do_ref, ga_ref, gb_ref, h_ref, wa_ref, wb_ref, wo_ref, o_ref):
    ya = jnp.dot(ao_ref[...], wa_ref[...], preferred_element_type=F32)
    yb = jnp.dot(do_ref[...], wb_ref[...], preferred_element_type=F32)
    mixed = (jax.nn.sigmoid(ga_ref[...].astype(F32)) * ya
             + jax.nn.sigmoid(gb_ref[...].astype(F32)) * yb)
    o_ref[...] = h_ref[...] + jnp.dot(mixed.astype(BF16), wo_ref[...],
                                      preferred_element_type=F32)


def _mix(ao, do, proj, h, wa, wb, wo, tm):
    lp = h.shape[0]
    blk = lambda j: pl.BlockSpec((tm, BLK), lambda m, j=j: (m, j))
    wspec = pl.BlockSpec((BLK, D_MODEL), lambda m: (0, 0))
    return pl.pallas_call(
        _mix_kernel,
        out_shape=jax.ShapeDtypeStruct((lp, D_MODEL), F32),
        grid=(lp // tm,),
        in_specs=[blk(0), blk(0), blk(7), blk(8), blk(0), wspec, wspec, wspec],
        out_specs=blk(0),
        compiler_params=_params(("parallel",)),
        name="mix",
    )(ao, do, proj, proj, h, wa, wb, wo)


def _ffn_kernel(h_ref, g_ref, wup_ref, cw_ref, wdn_ref, o_ref, fbuf_ref, *, tc):
    tm = h_ref.shape[0]
    halo = SUBLANES

    @pl.when(pl.program_id(0) == 0)
    def _():
        fbuf_ref[0:halo, :] = jnp.zeros((halo, 2 * D_FF), F32)

    @pl.when(pl.program_id(0) > 0)
    def _():
        fbuf_ref[0:halo, :] = fbuf_ref[tm:tm + halo, :]

    x = h_ref[...]
    ms = jnp.mean(x * x, axis=-1, keepdims=True)
    u = ((x * lax.rsqrt(ms + RMS_EPS)) * g_ref[...]).astype(BF16)
    for j in range(2 * D_FF // tc):
        sl = slice(j * tc, (j + 1) * tc)
        fbuf_ref[halo:, sl] = jnp.dot(u, wup_ref[:, sl], preferred_element_type=F32)

    def conv(sl):
        acc = None
        for t in range(FFN_CONV):
            off = halo - (FFN_CONV - 1) + t
            term = fbuf_ref[off:off + tm, sl] * cw_ref[t:t + 1, sl]
            acc = term if acc is None else acc + term
        return acc

    acc = x
    for j in range(D_FF // tc):
        gate = conv(slice(j * tc, (j + 1) * tc))
        up = conv(slice(D_FF + j * tc, D_FF + (j + 1) * tc))
        act = (_silu(gate) * up).astype(BF16)
        acc = acc + jnp.dot(act, wdn_ref[j * tc:(j + 1) * tc, :], preferred_element_type=F32)
    o_ref[...] = acc


def _ffn(h, g, wup, cw, wdn, tm, tc=256):
    lp = h.shape[0]
    once = pl.Buffered(1)
    return pl.pallas_call(
        functools.partial(_ffn_kernel, tc=tc),
        out_shape=jax.ShapeDtypeStruct((lp, D_MODEL), F32),
        grid=(lp // tm,),
        in_specs=[pl.BlockSpec((tm, D_MODEL), lambda m: (m, 0)),
                  pl.BlockSpec((1, D_MODEL), lambda m: (0, 0)),
                  pl.BlockSpec((D_MODEL, 2 * D_FF), lambda m: (0, 0), pipeline_mode=once),
                  pl.BlockSpec((FFN_CONV, 2 * D_FF), lambda m: (0, 0)),
                  pl.BlockSpec((D_FF, D_MODEL), lambda m: (0, 0), pipeline_mode=once)],
        out_specs=pl.BlockSpec((tm, D_MODEL), lambda m: (m, 0)),
        scratch_shapes=[pltpu.VMEM((tm + SUBLANES, 2 * D_FF), F32)],
        compiler_params=_params(("arbitrary",)),
        name="ffn",
    )(h, g, wup, cw, wdn)


def _regroup_w_in(w):
    a_cols = 3 * A_HEADS * A_DV
    dn_cols = DN_HEADS * (2 * DN_DK + DN_DV) + DN_HEADS * DN_DV
    o_small = a_cols + dn_cols
    w_main = jnp.concatenate([w[:, :o_small], w[:, o_small + 2 * DN_HEADS:]], axis=1)
    w_small = jnp.zeros((D_MODEL, 2 * LANES), w.dtype)
    w_small = w_small.at[:, 0:DN_HEADS].set(w[:, o_small:o_small + DN_HEADS])
    w_small = w_small.at[:, LANES:LANES + DN_HEADS].set(
        w[:, o_small + DN_HEADS:o_small + 2 * DN_HEADS])
    return w_main.astype(BF16), w_small.astype(BF16)


def _pad_lanes(v):
    return jnp.zeros((1, LANES), F32).at[0, :v.shape[0]].set(v.astype(F32))


def kernel(x, meta_tokens, mix_norm_g, w_in, q_norm_g, k_norm_g, lambda_q1, lambda_k1, lambda_q2, lambda_k2, attn_subln_g, dn_conv_w, dn_a_log, dn_dt_bias, dn_norm_g, w_branch_attn, w_branch_dn, w_out, ffn_norm_g, w_ffn_up, ffn_conv_w, w_ffn_down):
    batch, seq, _ = x.shape
    assert batch == 1
    depth = w_in.shape[0]
    length = N_META + seq
    lp = -(-length // ROW_ALIGN) * ROW_ALIGN
    tm = _pick_tile(lp, (768, 512))
    tm_ffn = _pick_tile(lp, (384, 256))
    qtab, ktab = _alibi_tables()
    h = jnp.concatenate([meta_tokens.astype(F32), x[0],
                         jnp.zeros((lp - length, D_MODEL), F32)], axis=0)
    row = lambda v: v.astype(F32)[None, :]
    for layer in range(depth):
        lam_init = 0.8 - 0.6 * math.exp(-0.3 * layer)
        w_main, w_small = _regroup_w_in(w_in[layer])
        proj, small = _inproj(h, row(mix_norm_g[layer]), w_main, w_small, tm)

        gq2 = jnp.tile(row(q_norm_g[layer]), (1, 2))
        gk2 = jnp.tile(row(k_norm_g[layer]), (1, 2))
        qn, ka, vb = _qkprep(proj, gq2, gk2, ktab, tm)
        bound = (A_DH ** 0.5) * LOG2E * jnp.max(jnp.abs(gq2)) * jnp.max(jnp.abs(gk2))
        ao = _attn(qn, ka, vb, bound, qtab, row(lambda_q1[layer]), row(lambda_k1[layer]),
                   row(lambda_q2[layer]), row(lambda_k2[layer]), row(attn_subln_g[layer]),
                   lam_init)

        dq, dk, dv, beta, gcum, gcum_t = _gdnprep(
            proj, small, dn_conv_w[layer].astype(F32), _pad_lanes(dn_a_log[layer]),
            _pad_lanes(dn_dt_bias[layer]), tm)
        do = _gdn(dq, dk, dv, beta, gcum, gcum_t, proj, row(dn_norm_g[layer]))

        h = _mix(ao, do, proj, h, w_branch_attn[layer].astype(BF16),
                 w_branch_dn[layer].astype(BF16), w_out[layer].astype(BF16), tm)
        h = _ffn(h, row(ffn_norm_g[layer]), w_ffn_up[layer].astype(BF16),
                 ffn_conv_w[layer].astype(F32), w_ffn_down[layer].astype(BF16), tm_ffn)
    return h[N_META:N_META + seq][None]
```

```python
import functools
import math

import jax
import jax.numpy as jnp
import ml_dtypes
import numpy as np
from jax import lax
from jax.experimental import pallas as pl
from jax.experimental.pallas import tpu as pltpu

D_MODEL = 1024
N_META = 16
A_HEADS = 8
A_DH = 64
A_DV = 2 * A_DH
DN_HEADS = 8
DN_DK = 128
DN_DV = 128
DN_CONV = 4
D_FF = 2816
FFN_CONV = 3
RMS_EPS = 1e-6
L2_EPS = 1e-6
ALIBI_MAX_EXP = 8.0

LANES = 128
SUBLANES = 8
ROW_ALIGN = 512
ATTN_TQ = 512
ATTN_TK = 256
ATTN_GROUP = 8
ATTN_HALF_GROUP = 4
SAFE_TQ = 256
SAFE_TK_BIG = 1024
GDN_CHUNK = 128
GDN_CHUNKS_PER_STEP = 2
BLK = 1024
LOG2E = 1.4426950408889634
VMEM_LIMIT = 52 * 1024 * 1024

FAST_EXP_BOUND = 60.0
ZERO_SCALE_EXP = 150.0

AUG_ROWB = 0
AUG_ROWA = 3
AUG_JR = 6
AUG_ONE = 9
AUG_JREL = 12
N_PIECES = 3
assert ATTN_TQ // ATTN_TK <= 2 and ATTN_TK == 256
AUG_BASE = (A_DH, 0)
N_QVAR = 1 + ATTN_TQ // ATTN_TK
QTAB_SLOPE_ROW = 2 * N_QVAR
QTAB_ROWS = 8

F32 = jnp.float32
BF16 = jnp.bfloat16
NT_DIMS = (((1,), (1,)), ((), ()))
TN_DIMS = (((0,), (0,)), ((), ()))


def _params(sem, vmem=VMEM_LIMIT):
    return pltpu.CompilerParams(dimension_semantics=sem, vmem_limit_bytes=vmem)


def _pick_tile(n, candidates):
    for c in candidates:
        if n % c == 0:
            return c
    raise ValueError(f"no tile in {candidates} divides {n}")


def _silu(x):
    return x * jax.nn.sigmoid(x)


def _bf16_pieces(x):
    out, r = [], np.float64(x)
    for _ in range(N_PIECES):
        p = float(np.float32(r).astype(ml_dtypes.bfloat16).astype(np.float32))
        out.append(p)
        r -= p
    return out


def _alibi_tables():
    qtab = np.zeros((A_HEADS, QTAB_ROWS, LANES), np.float32)
    ktab = np.zeros((2 * A_HEADS, LANES), np.float32)
    for h in range(A_HEADS):
        slope = 2.0 ** (-(ALIBI_MAX_EXP / A_HEADS) * (h + 1))
        pieces = _bf16_pieces(slope * LOG2E)
        qtab[h, QTAB_SLOPE_ROW, :] = np.float32(sum(pieces))
        for c in range(2):
            b = AUG_BASE[c]
            for i, p in enumerate(pieces):
                ktab[2 * h + c, b + AUG_ROWB + i] = -p
                ktab[2 * h + c, b + AUG_ROWA + i] = -float(ATTN_TK) * p
                ktab[2 * h + c, b + AUG_ONE + i] = 1.0
                qtab[h, N_QVAR * c, b + AUG_JR + i] = -p
                for d in range(ATTN_TQ // ATTN_TK):
                    qtab[h, N_QVAR * c + 1 + d, b + AUG_JREL + i] = p
                    qtab[h, N_QVAR * c + 1 + d, b + AUG_ONE + i] = p * ATTN_TK * d
    return jnp.asarray(qtab), jnp.asarray(ktab)


def _inproj_kernel(h_ref, g_ref, w_ref, ws_ref, o_ref, os_ref, u_ref):
    @pl.when(pl.program_id(1) == 0)
    def _():
        x = h_ref[...]
        ms = jnp.mean(x * x, axis=-1, keepdims=True)
        u = ((x * lax.rsqrt(ms + RMS_EPS)) * g_ref[...]).astype(BF16)
        u_ref[...] = u
        os_ref[...] = jnp.dot(u, ws_ref[...], preferred_element_type=F32)

    o_ref[...] = jnp.dot(u_ref[...], w_ref[...], preferred_element_type=F32).astype(o_ref.dtype)


def _inproj(h, g, w_main, w_small, tm, tn=BLK):
    lp = h.shape[0]
    n_main = w_main.shape[1]
    n_small = w_small.shape[1]
    return pl.pallas_call(
        _inproj_kernel,
        out_shape=(jax.ShapeDtypeStruct((lp, n_main), BF16),
                   jax.ShapeDtypeStruct((lp, n_small), F32)),
        grid=(lp // tm, n_main // tn),
        in_specs=[
            pl.BlockSpec((tm, D_MODEL), lambda m, n: (m, 0)),
            pl.BlockSpec((1, D_MODEL), lambda m, n: (0, 0)),
            pl.BlockSpec((D_MODEL, tn), lambda m, n: (0, n)),
            pl.BlockSpec((D_MODEL, n_small), lambda m, n: (0, 0)),
        ],
        out_specs=(
            pl.BlockSpec((tm, tn), lambda m, n: (m, n)),
            pl.BlockSpec((tm, n_small), lambda m, n: (m, 0)),
        ),
        scratch_shapes=[pltpu.VMEM((tm, D_MODEL), BF16)],
        compiler_params=_params(("parallel", "arbitrary")),
        name="inproj",
    )(h, g, w_main, w_small)


def _qkprep_kernel(aq_ref, ak_ref, av_ref, gq_ref, gk_ref, ktab_ref, q_out, k_out, v_out):
    tm = aq_ref.shape[0]
    lane = lax.broadcasted_iota(jnp.int32, (tm, LANES), 1)
    lo = lane < A_DH
    q_scale = (A_DH ** -0.5) * LOG2E

    def norm_pair(x, g):
        ss = x * x
        s_lo = jnp.sum(jnp.where(lo, ss, 0.0), axis=-1, keepdims=True)
        s_hi = jnp.sum(jnp.where(lo, 0.0, ss), axis=-1, keepdims=True)
        ms = jnp.where(lo, s_lo, s_hi) * (1.0 / A_DH)
        return (x * lax.rsqrt(ms + RMS_EPS)) * g

    j = (pl.program_id(0) * tm + lax.broadcasted_iota(jnp.int32, (tm, LANES), 0))
    jrel = jnp.bitwise_and(j, ATTN_TK - 1)
    jr = (ATTN_TK - 1) - jrel
    pos = []
    for c in range(2):
        la = lane - AUG_BASE[c]
        pick = lambda off, la=la: (la >= off) & (la < off + N_PIECES)
        pos.append(jnp.where(pick(AUG_JR), jr, jnp.where(pick(AUG_JREL), jrel, 0)).astype(F32))

    for hh in range(A_HEADS):
        sl = slice(hh * LANES, (hh + 1) * LANES)
        yq = norm_pair(aq_ref[:, sl].astype(F32), gq_ref[...]) * q_scale
        yk = norm_pair(ak_ref[:, sl].astype(F32), gk_ref[...])
        q_out[:, sl] = yq.astype(BF16)
        k_out[:, 2 * hh * LANES:(2 * hh + 1) * LANES] = jnp.where(
            lo, yk, pos[0] + ktab_ref[2 * hh:2 * hh + 1, :]).astype(BF16)
        k_out[:, (2 * hh + 1) * LANES:(2 * hh + 2) * LANES] = jnp.where(
            lo, pos[1] + ktab_ref[2 * hh + 1:2 * hh + 2, :], yk).astype(BF16)
    v_out[...] = av_ref[...]


def _qkprep(proj, gq2, gk2, ktab, tm):
    lp = proj.shape[0]
    blk = lambda j: pl.BlockSpec((tm, BLK), lambda m, j=j: (m, j))
    vec = pl.BlockSpec((1, LANES), lambda m: (0, 0))
    out = jax.ShapeDtypeStruct((lp, BLK), BF16)
    return pl.pallas_call(
        _qkprep_kernel,
        out_shape=(out, jax.ShapeDtypeStruct((lp, 2 * BLK), BF16), out),
        grid=(lp // tm,),
        in_specs=[blk(0), blk(1), blk(2), vec, vec,
                  pl.BlockSpec((2 * A_HEADS, LANES), lambda m: (0, 0))],
        out_specs=(blk(0), pl.BlockSpec((tm, 2 * BLK), lambda m: (m, 0)), blk(0)),
        compiler_params=_params(("parallel",)),
        name="qkprep",
    )(proj, proj, proj, gq2, gk2, ktab)


def _attn_finish(lq1_ref, lk1_ref, lq2_ref, lk2_ref, g_ref, o_ref, o1, o2, lam_init):
    lam = (jnp.exp(jnp.sum(lq1_ref[...] * lk1_ref[...], axis=-1, keepdims=True))
           - jnp.exp(jnp.sum(lq2_ref[...] * lk2_ref[...], axis=-1, keepdims=True))
           + lam_init)
    o = o1 - lam * o2
    ms = jnp.mean(o * o, axis=-1, keepdims=True)
    y = ((o * lax.rsqrt(ms + RMS_EPS)) * g_ref[...]) * (1.0 - lam_init)
    o_ref[...] = y.astype(BF16)


def _attn_fast_kernel(lq1_ref, lk1_ref, lq2_ref, lk2_ref, qtab_ref, q_ref, k_ref, v_ref, g_ref,
                      o_ref, qa_ref, l_ref, acc_ref, *, lam_init):
    tq, tk, grp = ATTN_TQ, ATTN_TK, ATTN_GROUP
    n_diag = tq // tk
    log_tk = int(math.log2(tk))
    h = pl.program_id(0)
    i = pl.program_id(1)
    row0 = i * tq
    tab = qtab_ref[0]
    slope2 = tab[QTAB_SLOPE_ROW:QTAB_SLOPE_ROW + 1, :]
    q = q_ref[...].astype(F32)
    lane = lax.broadcasted_iota(jnp.int32, (tq, LANES), 1)
    rowi = lax.broadcasted_iota(jnp.int32, (tq, LANES), 0)
    row_b = jnp.bitwise_and(rowi, tk - 1).astype(F32)
    row_a = jnp.right_shift(rowi, log_tk).astype(F32)
    tab16 = tab.astype(BF16)
    for c in range(2):
        content = (lane < A_DH) if c == 0 else (lane >= A_DH)
        la = lane - AUG_BASE[c]
        pick = lambda off, la=la: (la >= off) & (la < off + N_PIECES)
        rows = jnp.where(pick(AUG_ROWB), row_b, jnp.where(pick(AUG_ROWA), row_a, 0.0))
        base = jnp.where(content, q, rows).astype(BF16)
        for variant in range(N_QVAR):
            qa_ref[N_QVAR * c + variant] = base + tab16[N_QVAR * c + variant:N_QVAR * c + variant + 1, :]

    def scores(c, variant, start, rows=slice(None)):
        s = lax.dot_general(qa_ref[N_QVAR * c + variant, rows, :],
                            k_ref[pl.ds(start, tk), c * LANES:(c + 1) * LANES],
                            NT_DIMS, preferred_element_type=F32)
        return s

    slope2_tile = jnp.concatenate([slope2] * (tk // LANES), axis=1)

    def far_tiles(start, n_tiles, scale_dist):
        scale = jnp.exp2(-slope2 * scale_dist.astype(F32))
        for c in range(2):
            pv = None
            lsum = None
            for t in range(n_tiles):
                s = scores(c, 0, start + t * tk)
                after = n_tiles - 1 - t
                if after:
                    s = s - slope2_tile * float(tk * after)
                p = jnp.exp2(s)
                part = p[:, :LANES] + p[:, LANES:]
                lsum = part if lsum is None else lsum + part
                d = jnp.dot(p.astype(BF16), v_ref[pl.ds(start + t * tk, tk), :],
                            preferred_element_type=F32)
                pv = d if pv is None else pv + d
            acc_ref[c] += pv * scale
            l_ref[c] += lsum * scale

    l_ref[...] = jnp.zeros(l_ref.shape, F32)
    acc_ref[...] = jnp.zeros(acc_ref.shape, F32)

    assert ALIBI_MAX_EXP == A_HEADS
    zero_dist = jnp.left_shift(jnp.int32(math.ceil(ZERO_SCALE_EXP / LOG2E * 2.0)), h)
    avail = jnp.right_shift(row0, log_tk)
    first_tile = jnp.right_shift(jnp.maximum(row0 - zero_dist, 0), log_tk)
    need = avail - first_tile
    half = ATTN_HALF_GROUP
    n_groups = jnp.minimum((need + grp - half - 1) // grp, avail // grp)
    group_tile0 = avail - n_groups * grp
    left = jnp.maximum(group_tile0 - first_tile, 0)
    n_half = jnp.where((left > 0) & (group_tile0 >= half), 1, 0)
    half_tile0 = group_tile0 - n_half * half
    n_single = jnp.maximum(half_tile0 - first_tile, 0)

    def single_body(js, carry):
        start = pl.multiple_of((first_tile + js) * tk, tk)
        far_tiles(start, 1, row0 - (start + tk - 1))
        return carry

    lax.fori_loop(0, n_single, single_body, 0)

    n_odd = jnp.bitwise_and(n_groups, 1)

    def group_body(jg, carry):
        start = pl.multiple_of(group_tile0 * tk, tk)
        far_tiles(start, grp, row0 - (start + grp * tk - 1))
        return carry

    lax.fori_loop(0, n_odd, group_body, 0)

    def pair_body(jp, carry):
        start = pl.multiple_of((group_tile0 + (n_odd + 2 * jp) * grp) * tk, tk)
        far_tiles(start, 2 * grp, row0 - (start + 2 * grp * tk - 1))
        return carry

    lax.fori_loop(0, jnp.right_shift(n_groups, 1), pair_body, 0)

    def diag_tiles():
        for d in range(n_diag):
            rows = slice(d * tk, tq)
            n_rows = tq - d * tk
            keep = (lax.broadcasted_iota(jnp.int32, (1, tk), 1)
                    <= lax.broadcasted_iota(jnp.int32, (n_rows, 1), 0))
            start = pl.multiple_of(row0 + d * tk, tk)
            for c in range(2):
                s = jnp.where(keep, scores(c, 1 + d, start, rows), -jnp.inf)
                p = jnp.exp2(s)
                acc_ref[c, rows, :] += jnp.dot(p.astype(BF16), v_ref[pl.ds(start, tk), :],
                                               preferred_element_type=F32)
                l_ref[c, rows, :] += p[:, :LANES] + p[:, LANES:]

    def finish():
        o1 = acc_ref[0] / jnp.sum(l_ref[0], axis=-1, keepdims=True)
        o2 = acc_ref[1] / jnp.sum(l_ref[1], axis=-1, keepdims=True)
        _attn_finish(lq1_ref, lk1_ref, lq2_ref, lk2_ref, g_ref, o_ref, o1, o2, lam_init)

    @pl.when(n_half == 1)
    def _():
        start = pl.multiple_of(half_tile0 * tk, tk)
        far_tiles(start, half, row0 - (start + half * tk - 1))
        diag_tiles()
        finish()

    @pl.when(n_half == 0)
    def _():
        diag_tiles()
        finish()


def _attn_safe_kernel(lq1_ref, lk1_ref, lq2_ref, lk2_ref, q_ref, k_ref, v_ref, g_ref, o_ref,
                      m_ref, l_ref, acc_ref, *, lam_init):
    tq, tb = SAFE_TQ, SAFE_TK_BIG
    h = pl.program_id(0)
    i = pl.program_id(1)
    slope2 = jnp.exp2(-(ALIBI_MAX_EXP / A_HEADS) * (h + 1).astype(F32)) * LOG2E
    q = q_ref[...].astype(F32)
    lane = lax.broadcasted_iota(jnp.int32, (tq, LANES), 1)
    q_maps = (jnp.where(lane < A_DH, q, 0.0).astype(BF16), jnp.where(lane >= A_DH, q, 0.0).astype(BF16))
    m_ref[...] = jnp.full(m_ref.shape, -jnp.inf, F32)
    l_ref[...] = jnp.zeros(l_ref.shape, F32)
    acc_ref[...] = jnp.zeros(acc_ref.shape, F32)
    row0 = i * tq

    def block(start, size, masked):
        v = v_ref[pl.ds(start, size), :]
        col = lax.broadcasted_iota(jnp.int32, (1, size), 1) + (start - row0)
        bias = slope2 * col.astype(F32)
        if masked:
            keep = col <= lax.broadcasted_iota(jnp.int32, (tq, 1), 0)
        for c in range(2):
            k = k_ref[pl.ds(start, size), c * LANES:(c + 1) * LANES]
            s = lax.dot_general(q_maps[c], k, NT_DIMS, preferred_element_type=F32) + bias
            if masked:
                s = jnp.where(keep, s, -jnp.inf)
            m_prev = m_ref[c]
            m_new = jnp.maximum(m_prev, jnp.max(s, axis=-1, keepdims=True))
            alpha = jnp.exp2(m_prev - m_new)
            p = jnp.exp2(s - m_new)
            l_ref[c] = alpha * l_ref[c] + jnp.sum(p, axis=-1, keepdims=True)
            acc_ref[c] = alpha * acc_ref[c] + jnp.dot(p.astype(BF16), v,
                                                      preferred_element_type=F32)
            m_ref[c] = m_new

    n_big = row0 // tb

    def big_body(jb, carry):
        block(pl.multiple_of(jb * tb, tb), tb, False)
        return carry

    lax.fori_loop(0, n_big, big_body, 0)
    base = n_big * tb
    n_small = (row0 - base) // tq

    def small_body(js, carry):
        block(pl.multiple_of(base + js * tq, tq), tq, False)
        return carry

    lax.fori_loop(0, n_small, small_body, 0)
    block(pl.multiple_of(row0, tq), tq, True)
    _attn_finish(lq1_ref, lk1_ref, lq2_ref, lk2_ref, g_ref, o_ref,
                 acc_ref[0] / l_ref[0], acc_ref[1] / l_ref[1], lam_init)


def _attn(qn, ka, vb, bound, qtab, lq1, lk1, lq2, lk2, g_sub, lam_init):
    lp = qn.shape[0]
    lamspec = pl.BlockSpec((1, A_DH), lambda h, i: (0, 0))
    lams = [lamspec] * 4

    def common(tq):
        return dict(
            out_shape=jax.ShapeDtypeStruct((lp, A_HEADS * A_DV), BF16),
            grid=(A_HEADS, lp // tq),
            out_specs=pl.BlockSpec((tq, A_DV), lambda h, i: (i, h)),
            compiler_params=_params(("parallel", "arbitrary")),
        )

    def data_specs(tq):
        return [pl.BlockSpec((tq, LANES), lambda h, i: (i, h)),
                pl.BlockSpec((lp, 2 * LANES), lambda h, i: (0, h)),
                pl.BlockSpec((lp, LANES), lambda h, i: (0, h)),
                pl.BlockSpec((1, A_DV), lambda h, i: (0, 0))]

    def fast(qn, ka, vb):
        tq = ATTN_TQ
        return pl.pallas_call(
            functools.partial(_attn_fast_kernel, lam_init=lam_init),
            in_specs=(lams + [pl.BlockSpec((1, QTAB_ROWS, LANES), lambda h, i: (h, 0, 0))]
                      + data_specs(tq)),
            scratch_shapes=[pltpu.VMEM((2 * N_QVAR, tq, LANES), BF16),
                            pltpu.VMEM((2, tq, LANES), F32), pltpu.VMEM((2, tq, A_DV), F32)],
            name="attn_fast", **common(tq),
        )(lq1, lk1, lq2, lk2, qtab, qn, ka, vb, g_sub)

    def safe(qn, ka, vb):
        tq = SAFE_TQ
        return pl.pallas_call(
            functools.partial(_attn_safe_kernel, lam_init=lam_init),
            in_specs=lams + data_specs(tq),
            scratch_shapes=[pltpu.VMEM((2, tq, 1), F32), pltpu.VMEM((2, tq, 1), F32),
                            pltpu.VMEM((2, tq, A_DV), F32)],
            name="attn_safe", **common(tq),
        )(lq1, lk1, lq2, lk2, qn, ka, vb, g_sub)

    return lax.cond(bound <= FAST_EXP_BOUND, fast, safe, qn, ka, vb)


def _gdnprep_kernel(xq_ref, xk_ref, xv_ref, cw_ref, sm_ref, alog_ref, dtb_ref,
                    q_out, k_out, v_out, beta_out, gcum_out, gcumt_out, xbuf_ref):
    tm = xq_ref.shape[0]
    halo = SUBLANES

    @pl.when(pl.program_id(0) == 0)
    def _():
        xbuf_ref[0:halo, :] = jnp.zeros((halo, 3 * BLK), F32)

    xbuf_ref[halo:, 0:BLK] = xq_ref[...].astype(F32)
    xbuf_ref[halo:, BLK:2 * BLK] = xk_ref[...].astype(F32)
    xbuf_ref[halo:, 2 * BLK:] = xv_ref[...].astype(F32)

    outs = (q_out, k_out, v_out)
    for s in range(3 * BLK // LANES):
        sl = slice(s * LANES, (s + 1) * LANES)
        xw = xbuf_ref[:, sl]
        acc = xw[halo:] * cw_ref[DN_CONV - 1:DN_CONV, sl]
        for back in range(1, DN_CONV):
            shifted = pltpu.roll(xw, back, axis=0)[halo:]
            acc = acc + shifted * cw_ref[DN_CONV - 1 - back:DN_CONV - back, sl]
        y = _silu(acc)
        part, hh = divmod(s, DN_HEADS)
        if part < 2:
            y = y * lax.rsqrt(jnp.sum(y * y, axis=-1, keepdims=True) + L2_EPS)
            if part == 0:
                y = y * (DN_DK ** -0.5)
        outs[part][:, hh * LANES:(hh + 1) * LANES] = y.astype(BF16)

    xbuf_ref[0:halo, :] = xbuf_ref[tm:tm + halo, :]

    beta_out[...] = jax.nn.sigmoid(sm_ref[:, 0:LANES])
    g = -jnp.exp(alog_ref[...]) * jax.nn.softplus(sm_ref[:, LANES:] + dtb_ref[...])
    g1 = g.astype(BF16)
    r1 = g - g1.astype(F32)
    g2 = r1.astype(BF16)
    g3 = (r1 - g2.astype(F32)).astype(BF16)
    ii = lax.broadcasted_iota(jnp.int32, (GDN_CHUNK, GDN_CHUNK), 0)
    jj = lax.broadcasted_iota(jnp.int32, (GDN_CHUNK, GDN_CHUNK), 1)
    tril = (ii >= jj).astype(BF16)
    for cidx in range(tm // GDN_CHUNK):
        rs = slice(cidx * GDN_CHUNK, (cidx + 1) * GDN_CHUNK)
        gc = (jnp.dot(tril, g1[rs], preferred_element_type=F32)
              + jnp.dot(tril, g2[rs], preferred_element_type=F32)
              + jnp.dot(tril, g3[rs], preferred_element_type=F32))
        gcum_out[rs, :] = gc
        gcumt_out[:, rs] = gc.T[0:SUBLANES, :]


def _gdnprep(proj, small, conv_w, alog, dtb, tm):
    lp = proj.shape[0]
    blk = lambda j: pl.BlockSpec((tm, BLK), lambda m, j=j: (m, j))
    row = pl.BlockSpec((tm, LANES), lambda m: (m, 0))
    vec = pl.BlockSpec((1, LANES), lambda m: (0, 0))
    o16 = jax.ShapeDtypeStruct((lp, BLK), BF16)
    o32 = jax.ShapeDtypeStruct((lp, LANES), F32)
    return pl.pallas_call(
        _gdnprep_kernel,
        out_shape=(o16, o16, o16, o32, o32, jax.ShapeDtypeStruct((SUBLANES, lp), F32)),
        grid=(lp // tm,),
        in_specs=[blk(3), blk(4), blk(5),
                  pl.BlockSpec((DN_CONV, 3 * BLK), lambda m: (0, 0)),
                  pl.BlockSpec((tm, 2 * LANES), lambda m: (m, 0)), vec, vec],
        out_specs=(blk(0), blk(0), blk(0), row, row,
                   pl.BlockSpec((SUBLANES, tm), lambda m: (0, m))),
        scratch_shapes=[pltpu.VMEM((tm + SUBLANES, 3 * BLK), F32)],
        compiler_params=_params(("arbitrary",)),
        name="gdnprep",
    )(proj, proj, proj, conv_w, small, alog, dtb)


def _gdn_kernel(q_ref, k_ref, v_ref, beta_ref, gc_ref, gt_ref, z_ref, gn_ref, o_ref, s_ref):
    c = GDN_CHUNK
    heads = range(DN_HEADS)

    @pl.when(pl.program_id(0) == 0)
    def _():
        s_ref[...] = jnp.zeros(s_ref.shape, F32)

    ii = lax.broadcasted_iota(jnp.int32, (c, c), 0)
    jj = lax.broadcasted_iota(jnp.int32, (c, c), 1)
    ge = ii >= jj
    gt = ii > jj
    eye = (ii == jj).astype(F32)
    blk_id = lambda t: (jnp.right_shift(ii, t), jnp.right_shift(jj, t))
    same = lambda t: blk_id(t)[0] == blk_id(t)[1]
    pair_mask = same(1)
    merge_masks = [same(t + 1) & jnp.logical_not(same(t)) for t in range(1, int(math.log2(c)))]

    def mm(a, b):
        return jnp.dot(a.astype(BF16), b.astype(BF16), preferred_element_type=F32)

    units = [(ci, hh) for ci in range(GDN_CHUNKS_PER_STEP) for hh in heads]
    rows = lambda u: slice(u[0] * c, (u[0] + 1) * c)
    cols = lambda u: slice(u[1] * LANES, (u[1] + 1) * LANES)
    q16 = {u: q_ref[rows(u), cols(u)] for u in units}
    k16 = {u: k_ref[rows(u), cols(u)] for u in units}
    k = {u: k16[u].astype(F32) for u in units}
    beta = {u: beta_ref[rows(u), u[1]:u[1] + 1] for u in units}
    g_col = {u: gc_ref[rows(u), u[1]:u[1] + 1] for u in units}
    g_last = {u: gc_ref[(u[0] + 1) * c - 1:(u[0] + 1) * c, u[1]:u[1] + 1] for u in units}
    decay = {u: jnp.exp(jnp.where(ge, g_col[u] - gt_ref[u[1]:u[1] + 1, rows(u)], -jnp.inf))
             for u in units}
    e_g = {u: jnp.exp(g_col[u]) for u in units}
    kb = {u: k[u] * beta[u] for u in units}
    vb = {u: v_ref[rows(u), cols(u)].astype(F32) * beta[u] for u in units}
    kq = {u: lax.dot_general(jnp.concatenate([kb[u].astype(BF16), q16[u]], axis=0), k16[u],
                             NT_DIMS, preferred_element_type=F32) for u in units}
    a = {u: jnp.where(gt, kq[u][:c] * decay[u], 0.0) for u in units}
    qk = {u: jnp.where(ge, kq[u][c:] * decay[u], 0.0) for u in units}
    x = {u: eye - jnp.where(pair_mask, a[u], 0.0) for u in units}
    for mask in merge_masks:
        y = {u: mm(jnp.where(mask, a[u], 0.0), x[u]) for u in units}
        x = {u: x[u] - mm(x[u], y[u]) for u in units}
    uw = {u: mm(x[u], jnp.concatenate([vb[u], kb[u] * e_g[u]], axis=1)) for u in units}
    wq = {u: jnp.concatenate([uw[u][:, DN_DV:], q16[u].astype(F32) * e_g[u]], axis=0) for u in units}
    kd = {u: (k[u] * jnp.exp(g_last[u] - g_col[u])).astype(BF16) for u in units}
    state = [s_ref[hh] for hh in heads]
    for ci in range(GDN_CHUNKS_PER_STEP):
        us = [(ci, hh) for hh in heads]
        ws = [mm(wq[u], state[u[1]]) for u in us]
        v_new = [uw[u][:, :DN_DV] - ws[u[1]][:c] for u in us]
        o = [ws[u[1]][c:] + mm(qk[u], v_new[u[1]]) for u in us]
        state = [state[u[1]] * jnp.exp(g_last[u]) + lax.dot_general(
            kd[u], v_new[u[1]].astype(BF16), TN_DIMS, preferred_element_type=F32) for u in us]
        for u in us:
            ou = o[u[1]]
            ms = jnp.mean(ou * ou, axis=-1, keepdims=True)
            y = (((ou * lax.rsqrt(ms + RMS_EPS)) * gn_ref[...])
                 * _silu(z_ref[rows(u), cols(u)].astype(F32)))
            o_ref[rows(u), cols(u)] = y.astype(BF16)
    for hh in heads:
        s_ref[hh] = state[hh]


def _gdn(qn, kn, vn, beta, gcum, gcum_t, proj, gn):
    lp = qn.shape[0]
    r = GDN_CHUNK * GDN_CHUNKS_PER_STEP
    blk = lambda j: pl.BlockSpec((r, BLK), lambda n, j=j: (n, j))
    row = pl.BlockSpec((r, LANES), lambda n: (n, 0))
    return pl.pallas_call(
        _gdn_kernel,
        out_shape=jax.ShapeDtypeStruct((lp, DN_HEADS * DN_DV), BF16),
        grid=(lp // r,),
        in_specs=[blk(0), blk(0), blk(0), row, row,
                  pl.BlockSpec((SUBLANES, r), lambda n: (0, n)),
                  blk(6), pl.BlockSpec((1, DN_DV), lambda n: (0, 0))],
        out_specs=blk(0),
        scratch_shapes=[pltpu.VMEM((DN_HEADS, DN_DK, DN_DV), F32)],
        compiler_params=_params(("arbitrary",)),
        name="gdn",
    )(qn, kn, vn, beta, gcum, gcum_t, proj, gn)


def _mix_kernel(ao_ref, do_ref, ga_ref, gb_ref, h_ref, wa_ref, wb_ref, wo_ref, o_ref):
    ya = jnp.dot(ao_ref[...], wa_ref[...], preferred_element_type=F32)
    yb = jnp.dot(do_ref[...], wb_ref[...], preferred_element_type=F32)
    mixed = (jax.nn.sigmoid(ga_ref[...].astype(F32)) * ya
             + jax.nn.sigmoid(gb_ref[...].astype(F32)) * yb)
    o_ref[...] = h_ref[...] + jnp.dot(mixed.astype(BF16), wo_ref[...],
                                      preferred_element_type=F32)


def _mix(ao, do, proj, h, wa, wb, wo, tm):
    lp = h.shape[0]
    blk = lambda j: pl.BlockSpec((tm, BLK), lambda m, j=j: (m, j))
    wspec = pl.BlockSpec((BLK, D_MODEL), lambda m: (0, 0))
    return pl.pallas_call(
        _mix_kernel,
        out_shape=jax.ShapeDtypeStruct((lp, D_MODEL), F32),
        grid=(lp // tm,),
        in_specs=[blk(0), blk(0), blk(7), blk(8), blk(0), wspec, wspec, wspec],
        out_specs=blk(0),
        compiler_params=_params(("parallel",)),
        name="mix",
    )(ao, do, proj, proj, h, wa, wb, wo)


def _ffn_kernel(ao_ref, do_ref, ga_ref, gb_ref, h_ref, wa_ref, wb_ref, wo_ref,
                g_ref, wup_ref, cw_ref, wdn_ref, o_ref, fbuf_ref, *, tc):
    tm = h_ref.shape[0]
    halo = SUBLANES

    @pl.when(pl.program_id(0) == 0)
    def _():
        fbuf_ref[0:halo, :] = jnp.zeros((halo, 2 * D_FF), F32)

    @pl.when(pl.program_id(0) > 0)
    def _():
        fbuf_ref[0:halo, :] = fbuf_ref[tm:tm + halo, :]

    ya = jnp.dot(ao_ref[...], wa_ref[...], preferred_element_type=F32)
    yb = jnp.dot(do_ref[...], wb_ref[...], preferred_element_type=F32)
    mixed = (jax.nn.sigmoid(ga_ref[...].astype(F32)) * ya
             + jax.nn.sigmoid(gb_ref[...].astype(F32)) * yb)
    x = h_ref[...] + jnp.dot(mixed.astype(BF16), wo_ref[...], preferred_element_type=F32)
    ms = jnp.mean(x * x, axis=-1, keepdims=True)
    u = ((x * lax.rsqrt(ms + RMS_EPS)) * g_ref[...]).astype(BF16)
    for j in range(2 * D_FF // tc):
        sl = slice(j * tc, (j + 1) * tc)
        fbuf_ref[halo:, sl] = jnp.dot(u, wup_ref[:, sl], preferred_element_type=F32)

    def conv(sl):
        acc = None
        for t in range(FFN_CONV):
            off = halo - (FFN_CONV - 1) + t
            term = fbuf_ref[off:off + tm, sl] * cw_ref[t:t + 1, sl]
            acc = term if acc is None else acc + term
        return acc

    acc = x
    for j in range(D_FF // tc):
        gate = conv(slice(j * tc, (j + 1) * tc))
        up = conv(slice(D_FF + j * tc, D_FF + (j + 1) * tc))
        act = (_silu(gate) * up).astype(BF16)
        acc = acc + jnp.dot(act, wdn_ref[j * tc:(j + 1) * tc, :], preferred_element_type=F32)
    o_ref[...] = acc


def _mix_ffn(ao, do, proj, h, wa, wb, wo, g, wup, cw, wdn, tm, tc=256):
    lp = h.shape[0]
    once = pl.Buffered(1)
    blk = lambda j: pl.BlockSpec((tm, BLK), lambda m, j=j: (m, j))
    wspec = pl.BlockSpec((BLK, D_MODEL), lambda m: (0, 0), pipeline_mode=once)
    return pl.pallas_call(
        functools.partial(_ffn_kernel, tc=tc),
        out_shape=jax.ShapeDtypeStruct((lp, D_MODEL), F32),
        grid=(lp // tm,),
        in_specs=[blk(0), blk(0), blk(7), blk(8), blk(0), wspec, wspec, wspec,
                  pl.BlockSpec((1, D_MODEL), lambda m: (0, 0)),
                  pl.BlockSpec((D_MODEL, 2 * D_FF), lambda m: (0, 0), pipeline_mode=once),
                  pl.BlockSpec((FFN_CONV, 2 * D_FF), lambda m: (0, 0)),
                  pl.BlockSpec((D_FF, D_MODEL), lambda m: (0, 0), pipeline_mode=once)],
        out_specs=pl.BlockSpec((tm, D_MODEL), lambda m: (m, 0)),
        scratch_shapes=[pltpu.VMEM((tm + SUBLANES, 2 * D_FF), F32)],
        compiler_params=_params(("arbitrary",)),
        name="mix_ffn",
    )(ao, do, proj, proj, h, wa, wb, wo, g, wup, cw, wdn)


def _regroup_w_in(w):
    a_cols = 3 * A_HEADS * A_DV
    dn_cols = DN_HEADS * (2 * DN_DK + DN_DV) + DN_HEADS * DN_DV
    o_small = a_cols + dn_cols
    w_main = jnp.concatenate([w[:, :o_small], w[:, o_small + 2 * DN_HEADS:]], axis=1)
    w_small = jnp.zeros((D_MODEL, 2 * LANES), w.dtype)
    w_small = w_small.at[:, 0:DN_HEADS].set(w[:, o_small:o_small + DN_HEADS])
    w_small = w_small.at[:, LANES:LANES + DN_HEADS].set(
        w[:, o_small + DN_HEADS:o_small + 2 * DN_HEADS])
    return w_main.astype(BF16), w_small.astype(BF16)


def _pad_lanes(v):
    return jnp.zeros((1, LANES), F32).at[0, :v.shape[0]].set(v.astype(F32))


def kernel(x, meta_tokens, mix_norm_g, w_in, q_norm_g, k_norm_g, lambda_q1, lambda_k1, lambda_q2, lambda_k2, attn_subln_g, dn_conv_w, dn_a_log, dn_dt_bias, dn_norm_g, w_branch_attn, w_branch_dn, w_out, ffn_norm_g, w_ffn_up, ffn_conv_w, w_ffn_down):
    batch, seq, _ = x.shape
    assert batch == 1
    depth = w_in.shape[0]
    length = N_META + seq
    lp = -(-length // ROW_ALIGN) * ROW_ALIGN
    tm = _pick_tile(lp, (768, 512))
    tm_ffn = _pick_tile(lp, (384, 256))
    qtab, ktab = _alibi_tables()
    h = jnp.concatenate([meta_tokens.astype(F32), x[0],
                         jnp.zeros((lp - length, D_MODEL), F32)], axis=0)
    row = lambda v: v.astype(F32)[None, :]
    for layer in range(depth):
        lam_init = 0.8 - 0.6 * math.exp(-0.3 * layer)
        w_main, w_small = _regroup_w_in(w_in[layer])
        proj, small = _inproj(h, row(mix_norm_g[layer]), w_main, w_small, tm)

        gq2 = jnp.tile(row(q_norm_g[layer]), (1, 2))
        gk2 = jnp.tile(row(k_norm_g[layer]), (1, 2))
        qn, ka, vb = _qkprep(proj, gq2, gk2, ktab, tm)
        bound = (A_DH ** 0.5) * LOG2E * jnp.max(jnp.abs(gq2)) * jnp.max(jnp.abs(gk2))
        ao = _attn(qn, ka, vb, bound, qtab, row(lambda_q1[layer]), row(lambda_k1[layer]),
                   row(lambda_q2[layer]), row(lambda_k2[layer]), row(attn_subln_g[layer]),
                   lam_init)

        dq, dk, dv, beta, gcum, gcum_t = _gdnprep(
            proj, small, dn_conv_w[layer].astype(F32), _pad_lanes(dn_a_log[layer]),
            _pad_lanes(dn_dt_bias[layer]), tm)
        do = _gdn(dq, dk, dv, beta, gcum, gcum_t, proj, row(dn_norm_g[layer]))

        h = _mix_ffn(ao, do, proj, h, w_branch_attn[layer].astype(BF16),
                     w_branch_dn[layer].astype(BF16), w_out[layer].astype(BF16),
                     row(ffn_norm_g[layer]), w_ffn_up[layer].astype(BF16),
                     ffn_conv_w[layer].astype(F32), w_ffn_down[layer].astype(BF16), tm_ffn)
    return h[N_META:N_META + seq][None]
```
